```python
import math
import jax, jax.numpy as jnp
from jax import lax
import numpy as np

D_MODEL = 1024
BATCH = 4
SEQ = 8192
DEPTH = 1

SSD_D_INNER = D_MODEL
SSD_HEADDIM = 64
SSD_N_HEADS = SSD_D_INNER // SSD_HEADDIM
SSD_N_GROUPS = 4
SSD_HPG = SSD_N_HEADS // SSD_N_GROUPS
SSD_D_STATE = 128
SSD_CONV = 4
SSD_CHUNK = 128
SSD_CONV_DIM = SSD_D_INNER + 2 * SSD_N_GROUPS * SSD_D_STATE
S5_WIDTH = D_MODEL // 2
S5_GROUP = 16
S5_N_GROUPS = S5_WIDTH // S5_GROUP
S5_STATE = 64
D_FF = 4 * D_MODEL
N_BRANCHES = 2
OFF_Z = 0
OFF_XBC = OFF_Z + SSD_D_INNER
OFF_DT = OFF_XBC + SSD_CONV_DIM
OFF_U = OFF_DT + SSD_N_HEADS
OFF_G = OFF_U + S5_WIDTH
D_IN_PROJ = OFF_G + N_BRANCHES * D_MODEL
EPS = 1e-6
DT_MIN = 1e-3
DT_MAX = 1e-1

kernel_name = "hybrid_ssd_s5_gated_block"


def rms_norm(x, g):
    xf = x.astype(jnp.float32)
    y = xf * lax.rsqrt(jnp.mean(xf * xf, axis=-1, keepdims=True) + EPS)
    return y.astype(x.dtype) * g


def causal_dwconv(u, w, b):
    y = lax.conv_general_dilated(
        u, w.astype(u.dtype)[:, None, :], window_strides=(1,),
        padding=[(SSD_CONV - 1, 0)], dimension_numbers=('NWC', 'WIO', 'NWC'),
        feature_group_count=u.shape[-1])
    return y + b


def ssd_chunked(xh, dt, a, bm, cm):
    bsz, seqlen = xh.shape[0], xh.shape[1]
    nc = seqlen // SSD_CHUNK
    q, g, r, p, n = SSD_CHUNK, SSD_N_GROUPS, SSD_HPG, SSD_HEADDIM, SSD_D_STATE
    x = xh.astype(jnp.float32).reshape(bsz, nc, q, g, r, p)
    dtc = dt.astype(jnp.float32).reshape(bsz, nc, q, g, r)
    bc = bm.astype(jnp.float32).reshape(bsz, nc, q, g, n)
    cc = cm.astype(jnp.float32).reshape(bsz, nc, q, g, n)
    la = dtc * a.astype(jnp.float32).reshape(g, r)
    la_cum = jnp.cumsum(la, axis=2)
    xdt = x * dtc[..., None]
    seg = la_cum[:, :, :, None] - la_cum[:, :, None]
    causal = jnp.tril(jnp.ones((q, q), dtype=bool))[:, :, None, None]
    decay = jnp.exp(jnp.where(causal, seg, -jnp.inf))
    cb = jnp.einsum('bctgn,bcsgn->bctsg', cc, bc)
    y_diag = jnp.einsum('bctsg,bctsgr,bcsgrp->bctgrp', cb, decay, xdt)
    decay_end = jnp.exp(la_cum[:, :, -1:] - la_cum)
    states = jnp.einsum('bcsgn,bcsgr,bcsgrp->bcgrpn', bc, decay_end, xdt)
    chunk_decay = jnp.exp(la_cum[:, :, -1])

    def step(carry, inp):
        st, dec = inp
        return carry * dec[..., None, None] + st, carry

    init = jnp.zeros((bsz, g, r, p, n), jnp.float32)
    _, prev = lax.scan(step, init, (jnp.moveaxis(states, 1, 0), jnp.moveaxis(chunk_decay, 1, 0)))
    prev = jnp.moveaxis(prev, 0, 1)
    y_off = jnp.einsum('bctgn,bcgrpn,bctgr->bctgrp', cc, prev, jnp.exp(la_cum))
    return (y_diag + y_off).reshape(bsz, seqlen, SSD_N_HEADS, p)


def s5_mixer(u, a_re, a_im, log_dt, b_re, b_im, c_re, c_im, d):
    bsz, seqlen = u.shape[0], u.shape[1]
    ug = u.astype(jnp.float32).reshape(bsz, seqlen, S5_N_GROUPS, S5_GROUP)
    dt = jnp.exp(log_dt.astype(jnp.float32))[:, None]
    a_re = a_re.astype(jnp.float32)
    a_im = a_im.astype(jnp.float32)
    mag = jnp.exp(a_re * dt)
    ab_re = mag * jnp.cos(a_im * dt)
    ab_im = mag * jnp.sin(a_im * dt)
    den = a_re * a_re + a_im * a_im
    nr = ab_re - 1.0
    ni = ab_im
    coef_re = (nr * a_re + ni * a_im) / den
    coef_im = (ni * a_re - nr * a_im) / den
    bb_re = coef_re[..., None] * b_re - coef_im[..., None] * b_im
    bb_im = coef_re[..., None] * b_im + coef_im[..., None] * b_re
    bu_re = jnp.einsum('blgk,gpk->blgp', ug, bb_re)
    bu_im = jnp.einsum('blgk,gpk->blgp', ug, bb_im)
    a_seq_re = jnp.broadcast_to(ab_re, (1, seqlen, S5_N_GROUPS, S5_STATE))
    a_seq_im = jnp.broadcast_to(ab_im, (1, seqlen, S5_N_GROUPS, S5_STATE))

    def combine(e1, e2):
        a1r, a1i, b1r, b1i = e1
        a2r, a2i, b2r, b2i = e2
        return (a2r * a1r - a2i * a1i,
                a2r * a1i + a2i * a1r,
                a2r * b1r - a2i * b1i + b2r,
                a2r * b1i + a2i * b1r + b2i)

    _, _, s_re, s_im = lax.associative_scan(combine, (a_seq_re, a_seq_im, bu_re, bu_im), axis=1)
    y = jnp.einsum('blgp,gkp->blgk', s_re, c_re) - jnp.einsum('blgp,gkp->blgk', s_im, c_im)
    return y.reshape(bsz, seqlen, S5_WIDTH) + d * u


def setup_inputs(seed: int = 0) -> dict:
    key = jax.random.key(seed)
    ks = jax.random.split(key, 32)
    f32 = jnp.float32
    nrm = lambda k, shp, s: jax.random.normal(k, shp, f32) * s
    x = jax.random.normal(ks[0], (BATCH, SEQ, D_MODEL), f32)
    norm_mix_g = 1.0 + nrm(ks[1], (D_MODEL,), 0.02)
    w_in = nrm(ks[2], (D_MODEL, D_IN_PROJ), D_MODEL ** -0.5)
    conv_w = nrm(ks[3], (SSD_CONV, SSD_CONV_DIM), SSD_CONV ** -0.5)
    conv_b = nrm(ks[4], (SSD_CONV_DIM,), 0.02)
    dt0 = jnp.exp(jax.random.uniform(ks[5], (SSD_N_HEADS,), f32, math.log(DT_MIN), math.log(DT_MAX)))
    dt_bias = dt0 + jnp.log(-jnp.expm1(-dt0))
    a_log = jnp.log(jax.random.uniform(ks[6], (SSD_N_HEADS,), f32, 1.0, 16.0))
    d_ssd = 1.0 + nrm(ks[7], (SSD_N_HEADS,), 0.02)
    ssd_norm_g = 1.0 + nrm(ks[8], (SSD_D_INNER,), 0.02)
    s5_a_re = -0.5 + nrm(ks[9], (S5_N_GROUPS, S5_STATE), 0.01)
    s5_a_im = (math.pi * jnp.arange(S5_STATE, dtype=f32))[None, :] + nrm(ks[10], (S5_N_GROUPS, S5_STATE), 0.01)
    s5_log_dt = jax.random.uniform(ks[11], (S5_N_GROUPS,), f32, math.log(DT_MIN), math.log(DT_MAX))
    s5_b_re = nrm(ks[12], (S5_N_GROUPS, S5_STATE, S5_GROUP), (2 * S5_GROUP) ** -0.5)
    s5_b_im = nrm(ks[13], (S5_N_GROUPS, S5_STATE, S5_GROUP), (2 * S5_GROUP) ** -0.5)
    s5_c_re = nrm(ks[14], (S5_N_GROUPS, S5_GROUP, S5_STATE), (2 * S5_STATE) ** -0.5)
    s5_c_im = nrm(ks[15], (S5_N_GROUPS, S5_GROUP, S5_STATE), (2 * S5_STATE) ** -0.5)
    s5_d = nrm(ks[16], (S5_WIDTH,), 1.0)
    s5_glu_w = nrm(ks[17], (S5_WIDTH, S5_WIDTH), S5_WIDTH ** -0.5)
    s5_glu_b = nrm(ks[18], (S5_WIDTH,), 0.02)
    w_branch = jnp.concatenate([
        nrm(ks[19], (SSD_D_INNER, D_MODEL), SSD_D_INNER ** -0.5),
        nrm(ks[20], (S5_WIDTH, D_MODEL), S5_WIDTH ** -0.5)], axis=0)
    w_out = nrm(ks[21], (D_MODEL, D_MODEL), D_MODEL ** -0.5)
    norm_mlp_g = 1.0 + nrm(ks[22], (D_MODEL,), 0.02)
    w_mlp_in = nrm(ks[23], (D_MODEL, D_FF), D_MODEL ** -0.5)
    w_mlp_out = nrm(ks[24], (D_FF, D_MODEL), D_FF ** -0.5)
    norm_final_g = 1.0 + nrm(ks[25], (D_MODEL,), 0.02)
    return {"x": x, "norm_mix_g": norm_mix_g, "w_in": w_in, "conv_w": conv_w, "conv_b": conv_b,
            "dt_bias": dt_bias, "a_log": a_log, "d_ssd": d_ssd, "ssd_norm_g": ssd_norm_g,
            "s5_a_re": s5_a_re, "s5_a_im": s5_a_im, "s5_log_dt": s5_log_dt,
            "s5_b_re": s5_b_re, "s5_b_im": s5_b_im, "s5_c_re": s5_c_re, "s5_c_im": s5_c_im,
            "s5_d": s5_d, "s5_glu_w": s5_glu_w, "s5_glu_b": s5_glu_b,
            "w_branch": w_branch, "w_out": w_out, "norm_mlp_g": norm_mlp_g,
            "w_mlp_in": w_mlp_in, "w_mlp_out": w_mlp_out, "norm_final_g": norm_final_g}


def reference(x, norm_mix_g, w_in, conv_w, conv_b, dt_bias, a_log, d_ssd, ssd_norm_g,
              s5_a_re, s5_a_im, s5_log_dt, s5_b_re, s5_b_im, s5_c_re, s5_c_im, s5_d,
              s5_glu_w, s5_glu_b, w_branch, w_out, norm_mlp_g, w_mlp_in, w_mlp_out,
              norm_final_g):
    bsz, seqlen = x.shape[0], x.shape[1]
    for _ in range(DEPTH):
        h = rms_norm(x, norm_mix_g)
        proj = h @ w_in
        z = proj[..., OFF_Z:OFF_XBC]
        xbc = proj[..., OFF_XBC:OFF_DT]
        dt_raw = proj[..., OFF_DT:OFF_U]
        u5 = proj[..., OFF_U:OFF_G]
        gates = jax.nn.sigmoid(proj[..., OFF_G:].astype(jnp.float32)).reshape(bsz, seqlen, N_BRANCHES, D_MODEL)

        xbc = jax.nn.silu(causal_dwconv(xbc, conv_w, conv_b))
        xs = xbc[..., :SSD_D_INNER].reshape(bsz, seqlen, SSD_N_HEADS, SSD_HEADDIM)
        bm = xbc[..., SSD_D_INNER:SSD_D_INNER + SSD_N_GROUPS * SSD_D_STATE].reshape(bsz, seqlen, SSD_N_GROUPS, SSD_D_STATE)
        cm = xbc[..., SSD_D_INNER + SSD_N_GROUPS * SSD_D_STATE:].reshape(bsz, seqlen, SSD_N_GROUPS, SSD_D_STATE)
        dt = jax.nn.softplus(dt_raw.astype(jnp.float32) + dt_bias)
        a = -jnp.exp(a_log.astype(jnp.float32))
        y_a = ssd_chunked(xs, dt, a, bm, cm) + d_ssd[:, None] * xs
        y_a = y_a.reshape(bsz, seqlen, SSD_D_INNER) * jax.nn.silu(z)
        y_a = rms_norm(y_a.reshape(bsz, seqlen, SSD_N_GROUPS, SSD_D_INNER // SSD_N_GROUPS), 1.0)
        y_a = y_a.reshape(bsz, seqlen, SSD_D_INNER) * ssd_norm_g

        y_b = s5_mixer(u5, s5_a_re, s5_a_im, s5_log_dt, s5_b_re, s5_b_im, s5_c_re, s5_c_im, s5_d)
        y_b = jax.nn.gelu(y_b)
        y_b = y_b * jax.nn.sigmoid(y_b @ s5_glu_w + s5_glu_b)

        p_a = y_a @ w_branch[:SSD_D_INNER]
        p_b = y_b @ w_branch[SSD_D_INNER:]
        merged = gates[:, :, 0] * p_a + gates[:, :, 1] * p_b
        x = x + (merged @ w_out).astype(x.dtype)

        h2 = rms_norm(x, norm_mlp_g)
        x = x + (jnp.square(jax.nn.relu(h2 @ w_mlp_in)) @ w_mlp_out).astype(x.dtype)
    return rms_norm(x, norm_final_g)
```

```python
import functools
import math

import jax
import jax.numpy as jnp
from jax import lax
from jax.experimental import pallas as pl
from jax.experimental.pallas import tpu as pltpu

F32 = jnp.float32
BF16 = jnp.bfloat16

D_MODEL = 1024
SSD_D_INNER = D_MODEL
SSD_HEADDIM = 64
SSD_N_HEADS = SSD_D_INNER // SSD_HEADDIM
SSD_N_GROUPS = 4
SSD_HPG = SSD_N_HEADS // SSD_N_GROUPS
SSD_D_STATE = 128
SSD_CONV = 4
SSD_CHUNK = 128
SSD_GROUP_W = SSD_D_INNER // SSD_N_GROUPS
SSD_BC_W = SSD_N_GROUPS * SSD_D_STATE
SSD_CONV_DIM = SSD_D_INNER + 2 * SSD_BC_W
S5_WIDTH = D_MODEL // 2
S5_GROUP = 16
S5_N_GROUPS = S5_WIDTH // S5_GROUP
S5_STATE = 64
S5_BLOCK = 16
S5_BLOCK_W = S5_BLOCK * S5_GROUP
D_FF = 4 * D_MODEL
N_BRANCHES = 2
OFF_Z = 0
OFF_XBC = OFF_Z + SSD_D_INNER
OFF_DT = OFF_XBC + SSD_CONV_DIM
OFF_U = OFF_DT + SSD_N_HEADS
OFF_G = OFF_U + S5_WIDTH
EPS = 1e-6

V7X_LANES = 128
V7X_SUBLANES = 8
V7X_VMEM_LIMIT_BYTES = 56 * 1024 * 1024

ROW_TILE = 512
SSD_ROW_TILE = 512


def _const_spec(shape):
    zeros = (0,) * len(shape)
    return pl.BlockSpec(shape, lambda *_: zeros, pipeline_mode=pl.Buffered(1))


def _rms(x):
    return x * lax.rsqrt(jnp.mean(x * x, axis=-1, keepdims=True) + EPS)


def _split3(x):
    hi = x.astype(BF16)
    r1 = x - hi.astype(F32)
    mid = r1.astype(BF16)
    lo = (r1 - mid.astype(F32)).astype(BF16)
    return hi, mid, lo


def _softplus(x):
    return jnp.maximum(x, 0.0) + jnp.log(1.0 + jnp.exp(-jnp.abs(x)))


def _s5_prep_kernel(are_ref, aim_ref, ldt_ref, bre_ref, bim_ref, cre_ref, cim_ref,
                    taps_ref, ere_ref, eim_ref, fre_ref, fim_ref, pwre_ref, pwim_ref):
    a_re = are_ref[...]
    a_im = aim_ref[...]
    dt = jnp.exp(ldt_ref[...])
    mag = jnp.exp(a_re * dt)
    ab_re = mag * jnp.cos(a_im * dt)
    ab_im = mag * jnp.sin(a_im * dt)
    den = a_re * a_re + a_im * a_im
    nr = ab_re - 1.0
    ni = ab_im
    coef_re = (nr * a_re + ni * a_im) / den
    coef_im = (ni * a_re - nr * a_im) / den
    b_re = bre_ref[...]
    b_im = bim_ref[...]
    bb_re = coef_re * b_re - coef_im * b_im
    bb_im = coef_re * b_im + coef_im * b_re
    c_re = cre_ref[...]
    c_im = cim_ref[...]

    p_re = jnp.ones_like(ab_re)
    p_im = jnp.zeros_like(ab_re)
    ca_re = []
    ca_im = []
    for tau in range(S5_BLOCK + 1):
        ca_re.append(c_re * p_re - c_im * p_im)
        ca_im.append(c_re * p_im + c_im * p_re)
        if tau < S5_BLOCK:
            s = S5_BLOCK - 1 - tau
            ere_ref[s] = p_re * bb_re - p_im * bb_im
            eim_ref[s] = p_re * bb_im + p_im * bb_re
        if tau >= 1:
            fre_ref[tau - 1] = ca_re[tau]
            fim_ref[tau - 1] = -ca_im[tau]
        if tau == S5_BLOCK:
            pwre_ref[...] = p_re
            pwim_ref[...] = p_im
        p_re, p_im = (p_re * ab_re - p_im * ab_im, p_re * ab_im + p_im * ab_re)

    car = jnp.concatenate(ca_re[:S5_BLOCK], axis=1)
    cai = jnp.concatenate(ca_im[:S5_BLOCK], axis=1)
    dn = (((2,), (2,)), ((0,), (0,)))
    taps = (lax.dot_general(bb_re, car, dn, precision=lax.Precision.HIGHEST,
                            preferred_element_type=F32)
            - lax.dot_general(bb_im, cai, dn, precision=lax.Precision.HIGHEST,
                              preferred_element_type=F32))
    taps_ref[...] = taps


def _s5_prep(a_re, a_im, log_dt, b_re, b_im, c_re, c_im):
    g = S5_N_GROUPS
    bt_re = jnp.transpose(b_re, (0, 2, 1))
    bt_im = jnp.transpose(b_im, (0, 2, 1))
    tab = jax.ShapeDtypeStruct((S5_BLOCK, g, S5_GROUP, S5_STATE), F32)
    return pl.pallas_call(
        _s5_prep_kernel,
        out_shape=(jax.ShapeDtypeStruct((g, S5_GROUP, S5_BLOCK_W), F32), tab, tab, tab, tab,
                   jax.ShapeDtypeStruct((g, 1, S5_STATE), F32),
                   jax.ShapeDtypeStruct((g, 1, S5_STATE), F32)),
        name="s5_prep",
    )(a_re.reshape(g, 1, S5_STATE), a_im.reshape(g, 1, S5_STATE), log_dt.reshape(g, 1, 1),
      bt_re, bt_im, c_re, c_im)


def _in_proj_kernel(x_ref, g_ref, wz_ref, wxbc_ref, wdt_ref, wdtt_ref, wu_ref, wg_ref,
                    z_ref, xbc_ref, dt_ref, dtt_ref, u_ref, gate_ref):
    h = (_rms(x_ref[...]) * g_ref[...]).astype(BF16)
    z_ref[...] = jnp.dot(h, wz_ref[...], preferred_element_type=F32).astype(BF16)
    xbc_ref[...] = jnp.dot(h, wxbc_ref[...], preferred_element_type=F32).astype(BF16)
    u_ref[...] = jnp.dot(h, wu_ref[...], preferred_element_type=F32).astype(BF16)
    gate_ref[...] = jax.nn.sigmoid(
        jnp.dot(h, wg_ref[...], preferred_element_type=F32)).astype(BF16)
    dt = jnp.dot(h, wdt_ref[...], preferred_element_type=F32)
    dt_ref[...] = dt[:, :SSD_N_HEADS]
    dtt = lax.dot_general(wdtt_ref[...], h, (((1,), (1,)), ((), ())),
                          preferred_element_type=F32)
    for c in range(dtt_ref.shape[0]):
        dtt_ref[c] = dtt[:, c * SSD_CHUNK:(c + 1) * SSD_CHUNK]


def _in_proj(x2, g_mix, w_in):
    t = x2.shape[0]
    tm = ROW_TILE
    wz = w_in[:, OFF_Z:OFF_XBC].astype(BF16)
    wxbc = w_in[:, OFF_XBC:OFF_DT].astype(BF16)
    wdt = w_in[:, OFF_DT:OFF_U]
    wdt_pad = jnp.pad(wdt, ((0, 0), (0, V7X_LANES - SSD_N_HEADS))).astype(BF16)
    wdtt = wdt.T.astype(BF16)
    wu = w_in[:, OFF_U:OFF_G].astype(BF16)
    wg = w_in[:, OFF_G:].astype(BF16)
    row = lambda w: pl.BlockSpec((tm, w), lambda i: (i, 0))
    return pl.pallas_call(
        _in_proj_kernel,
        grid=(t // tm,),
        in_specs=[row(D_MODEL), _const_spec((1, D_MODEL)), _const_spec(wz.shape),
                  _const_spec(wxbc.shape), _const_spec(wdt_pad.shape), _const_spec(wdtt.shape),
                  _const_spec(wu.shape), _const_spec(wg.shape)],
        out_specs=[row(SSD_D_INNER), row(SSD_CONV_DIM), row(SSD_N_HEADS),
                   pl.BlockSpec((tm // SSD_CHUNK, SSD_N_HEADS, SSD_CHUNK), lambda i: (i, 0, 0)),
                   row(S5_WIDTH), row(N_BRANCHES * D_MODEL)],
        out_shape=(jax.ShapeDtypeStruct((t, SSD_D_INNER), BF16),
                   jax.ShapeDtypeStruct((t, SSD_CONV_DIM), BF16),
                   jax.ShapeDtypeStruct((t, SSD_N_HEADS), F32),
                   jax.ShapeDtypeStruct((t // SSD_CHUNK, SSD_N_HEADS, SSD_CHUNK), F32),
                   jax.ShapeDtypeStruct((t, S5_WIDTH), BF16),
                   jax.ShapeDtypeStruct((t, N_BRANCHES * D_MODEL), BF16)),
        compiler_params=pltpu.CompilerParams(dimension_semantics=("arbitrary",),
                                             vmem_limit_bytes=V7X_VMEM_LIMIT_BYTES),
        name="in_proj",
    )(x2, g_mix.reshape(1, D_MODEL), wz, wxbc, wdt_pad, wdtt, wu, wg)


def _ssd_kernel(xbc_ref, z_ref, dt_ref, dtt_ref, cw_ref, cb_ref, dtbc_ref, dtbr_ref,
                ac_ref, ar_ref, dcol_ref, ng_ref, y_ref, xpad, xs_s, b_s, c_s, st_s):
    tm = xbc_ref.shape[0]
    nch = tm // SSD_CHUNK
    halo = V7X_SUBLANES
    first = pl.program_id(1) == 0

    @pl.when(first)
    def _():
        xpad[0:halo, :] = jnp.zeros((halo, SSD_CONV_DIM), F32)
        st_s[...] = jnp.zeros_like(st_s)

    @pl.when(jnp.logical_not(first))
    def _():
        xpad[0:halo, :] = xpad[tm:tm + halo, :]

    xpad[halo:halo + tm, :] = xbc_ref[...].astype(F32)

    for c in range(nch):
        base = c * SSD_CHUNK + halo - (SSD_CONV - 1)
        acc = cb_ref[...]
        for k in range(SSD_CONV):
            acc = acc + cw_ref[k:k + 1, :] * xpad[base + k:base + k + SSD_CHUNK, :]
        v = acc * jax.nn.sigmoid(acc)
        rows = slice(c * SSD_CHUNK, (c + 1) * SSD_CHUNK)
        xs_s[rows, :] = v[:, :SSD_D_INNER]
        b_s[rows, :] = v[:, SSD_D_INNER:SSD_D_INNER + SSD_BC_W].astype(BF16)
        c_s[rows, :] = v[:, SSD_D_INNER + SSD_BC_W:].astype(BF16)

    q = SSD_CHUNK
    row_i = lax.broadcasted_iota(jnp.int32, (q, q), 0)
    col_i = lax.broadcasted_iota(jnp.int32, (q, q), 1)
    causal = row_i >= col_i
    tri = causal.astype(BF16)
    tri_t = (row_i <= col_i).astype(BF16)
    lane_w = lax.broadcasted_iota(jnp.int32, (q, V7X_LANES), 1)
    left_half = lane_w < SSD_HEADDIM
    head_of_lane = lax.broadcasted_iota(jnp.int32, (q, SSD_GROUP_W), 1) // SSD_HEADDIM
    a_c = -jnp.exp(ac_ref[...])
    a_r = -jnp.exp(ar_ref[...])

    def chunk(c, carry):
        r0 = pl.multiple_of(c * q, q)
        dt_c = _softplus(dt_ref[pl.ds(r0, q), :] + dtbc_ref[...])
        dt_r = _softplus(dtt_ref[c] + dtbr_ref[...])
        la_c = dt_c * a_c
        la_r = dt_r * a_r
        cum_c = sum(jnp.dot(tri, p, preferred_element_type=F32) for p in _split3(la_c))
        cum_r = sum(jnp.dot(p, tri_t, preferred_element_type=F32) for p in _split3(la_r))

        for g in range(SSD_N_GROUPS):
            cg = c_s[pl.ds(r0, q), g * SSD_D_STATE:(g + 1) * SSD_D_STATE]
            bg = b_s[pl.ds(r0, q), g * SSD_D_STATE:(g + 1) * SSD_D_STATE]
            cb = lax.dot_general(cg, bg, (((1,), (1,)), ((), ())),
                                 preferred_element_type=F32)
            ws, ccb, dtb = [], [], []
            for r in range(SSD_HPG):
                h = g * SSD_HPG + r
                cc = jnp.broadcast_to(cum_c[:, h:h + 1], (q, q))
                cr = jnp.broadcast_to(cum_r[h:h + 1, :], (q, q))
                lm = jnp.exp(jnp.where(causal, cc - cr, -jnp.inf))
                ws.append((cb * lm).astype(BF16))
                ccb.append(cc)
                dtb.append(jnp.broadcast_to(dt_c[:, h:h + 1], (q, V7X_LANES)))
            wcat = jnp.concatenate(ws, axis=1)
            ccg = jnp.concatenate([jnp.where(left_half, ccb[0], ccb[1]),
                                   jnp.where(left_half, ccb[2], ccb[3])], axis=1)
            dtg = jnp.concatenate([jnp.where(left_half, dtb[0], dtb[1]),
                                   jnp.where(left_half, dtb[2], dtb[3])], axis=1)
            cols = slice(g * SSD_GROUP_W, (g + 1) * SSD_GROUP_W)
            xs_g = xs_s[pl.ds(r0, q), cols]
            xdt = xs_g * dtg
            cl = ccg[q - 1:q, :]
            xdtd = (xdt * jnp.exp(cl - ccg)).astype(BF16)
            xdt_b = xdt.astype(BF16)
            xbd = jnp.concatenate(
                [jnp.where(head_of_lane == r, xdt_b, jnp.zeros_like(xdt_b))
                 for r in range(SSD_HPG)], axis=0)
            y = jnp.dot(wcat, xbd, preferred_element_type=F32)
            prev = st_s[:, cols]
            y = y + jnp.dot(cg, prev.astype(BF16), preferred_element_type=F32) * jnp.exp(ccg)
            st_s[:, cols] = prev * jnp.exp(cl) + lax.dot_general(
                bg, xdtd, (((0,), (0,)), ((), ())), preferred_element_type=F32)
            y = y + dcol_ref[:, cols] * xs_g
            zg = z_ref[pl.ds(r0, q), cols].astype(F32)
            y = y * (zg * jax.nn.sigmoid(zg))
            y_ref[pl.ds(r0, q), cols] = (_rms(y) * ng_ref[:, cols]).astype(BF16)
        return carry

    lax.fori_loop(0, nch, chunk, 0)


def _ssd(xbc, z, dt, dtt, conv_w, conv_b, dt_bias, a_log, d_ssd, ssd_norm_g, bsz, seqlen):
    tm = SSD_ROW_TILE
    nt = seqlen // tm
    nch = tm // SSD_CHUNK
    h = SSD_N_HEADS
    row = lambda w: pl.BlockSpec((tm, w), lambda b, i: (b * nt + i, 0))
    return pl.pallas_call(
        _ssd_kernel,
        grid=(bsz, nt),
        in_specs=[row(SSD_CONV_DIM), row(SSD_D_INNER), row(h),
                  pl.BlockSpec((nch, h, SSD_CHUNK), lambda b, i: (b * nt + i, 0, 0)),
                  _const_spec((SSD_CONV, SSD_CONV_DIM)), _const_spec((1, SSD_CONV_DIM)),
                  _const_spec((1, h)), _const_spec((h, 1)), _const_spec((1, h)),
                  _const_spec((h, 1)), _const_spec((1, SSD_D_INNER)),
                  _const_spec((1, SSD_D_INNER))],
        out_specs=row(SSD_D_INNER),
        out_shape=jax.ShapeDtypeStruct((bsz * seqlen, SSD_D_INNER), BF16),
        scratch_shapes=[pltpu.VMEM((tm + V7X_SUBLANES, SSD_CONV_DIM), F32),
                        pltpu.VMEM((tm, SSD_D_INNER), F32),
                        pltpu.VMEM((tm, SSD_BC_W), BF16),
                        pltpu.VMEM((tm, SSD_BC_W), BF16),
                        pltpu.VMEM((SSD_D_STATE, SSD_D_INNER), F32)],
        compiler_params=pltpu.CompilerParams(dimension_semantics=("arbitrary", "arbitrary"),
                                             vmem_limit_bytes=V7X_VMEM_LIMIT_BYTES),
        name="ssd",
    )(xbc, z, dt, dtt, conv_w, conv_b.reshape(1, -1), dt_bias.reshape(1, h),
      dt_bias.reshape(h, 1), a_log.reshape(1, h), a_log.reshape(h, 1),
      jnp.repeat(d_ssd, SSD_HEADDIM).reshape(1, SSD_D_INNER), ssd_norm_g.reshape(1, -1))


def _s5_kernel(u_ref, m_ref, e_ref, f_ref, pre_ref, pim_ref, d_ref, y_ref, sre, sim):
    npair = e_ref.shape[0]
    nblk = u_ref.shape[1]
    hw = V7X_LANES

    for pr in range(npair):
        ucat = jnp.concatenate([u_ref[2 * pr], u_ref[2 * pr + 1]], axis=1)
        sl = jnp.dot(ucat, e_ref[pr], preferred_element_type=F32)
        sre[:, pr * hw:(pr + 1) * hw] = sl[:, :hw]
        sim[:, pr * hw:(pr + 1) * hw] = sl[:, hw:]

    a_re = pre_ref[...]
    a_im = pim_ref[...]

    def step(r, carry):
        c_re, c_im = carry
        l_re = sre[pl.ds(r, 1), :]
        l_im = sim[pl.ds(r, 1), :]
        sre[pl.ds(r, 1), :] = c_re
        sim[pl.ds(r, 1), :] = c_im
        return (a_re * c_re - a_im * c_im + l_re, a_re * c_im + a_im * c_re + l_im)

    zero = jnp.zeros_like(a_re)
    lax.fori_loop(0, nblk, step, (zero, zero))

    for pr in range(npair):
        sp = jnp.concatenate([sre[:, pr * hw:(pr + 1) * hw],
                              sim[:, pr * hw:(pr + 1) * hw]], axis=1).astype(BF16)
        cross = jnp.dot(sp, f_ref[pr], preferred_element_type=F32)
        for j in range(2):
            gi = 2 * pr + j
            u = u_ref[gi]
            y = (jnp.dot(u, m_ref[gi], preferred_element_type=F32)
                 + cross[:, j * S5_BLOCK_W:(j + 1) * S5_BLOCK_W]
                 + d_ref[gi] * u.astype(F32))
            y_ref[gi] = jax.nn.gelu(y).astype(BF16)


def _s5(ug, m, e, f, pw_re, pw_im, d_tile, bsz):
    g, nb_total, w = ug.shape
    nblk = nb_total // bsz
    gh = g // 2
    return pl.pallas_call(
        _s5_kernel,
        grid=(bsz, 2),
        in_specs=[pl.BlockSpec((gh, nblk, w), lambda b, h: (h, b, 0)),
                  pl.BlockSpec((gh, w, w), lambda b, h: (h, 0, 0)),
                  pl.BlockSpec((gh // 2, 2 * w, 2 * V7X_LANES), lambda b, h: (h, 0, 0)),
                  pl.BlockSpec((gh // 2, 2 * V7X_LANES, 2 * w), lambda b, h: (h, 0, 0)),
                  pl.BlockSpec((None, 1, gh * S5_STATE), lambda b, h: (h, 0, 0)),
                  pl.BlockSpec((None, 1, gh * S5_STATE), lambda b, h: (h, 0, 0)),
                  pl.BlockSpec((gh, 1, w), lambda b, h: (h, 0, 0))],
        out_specs=pl.BlockSpec((gh, nblk, w), lambda b, h: (h, b, 0)),
        out_shape=jax.ShapeDtypeStruct(ug.shape, BF16),
        scratch_shapes=[pltpu.VMEM((nblk, gh * S5_STATE), F32),
                        pltpu.VMEM((nblk, gh * S5_STATE), F32)],
        compiler_params=pltpu.CompilerParams(dimension_semantics=("arbitrary", "arbitrary"),
                                             vmem_limit_bytes=V7X_VMEM_LIMIT_BYTES),
        name="s5",
    )(ug, m, e, f, pw_re, pw_im, d_tile)


def _s5_tables(taps, e_re, e_im, f_re, f_im, pw_re, pw_im, s5_d):
    g, q, w = S5_N_GROUPS, S5_BLOCK, S5_BLOCK_W
    m = jnp.stack([jnp.pad(taps[:, :, :w - S5_GROUP * s], ((0, 0), (0, 0), (S5_GROUP * s, 0)))
                   for s in range(q)], axis=1).reshape(g, w, w).astype(BF16)
    eg_re = jnp.transpose(e_re, (1, 0, 2, 3)).reshape(g // 2, 2, w, S5_STATE)
    eg_im = jnp.transpose(e_im, (1, 0, 2, 3)).reshape(g // 2, 2, w, S5_STATE)
    zero = jnp.zeros_like(eg_re[:, 0])
    e_top = jnp.concatenate([eg_re[:, 0], zero, eg_im[:, 0], zero], axis=-1)
    e_bot = jnp.concatenate([zero, eg_re[:, 1], zero, eg_im[:, 1]], axis=-1)
    e = jnp.concatenate([e_top, e_bot], axis=1).astype(BF16)
    fg_re = jnp.transpose(f_re, (1, 3, 0, 2)).reshape(g // 2, 2, S5_STATE, w)
    fg_im = jnp.transpose(f_im, (1, 3, 0, 2)).reshape(g // 2, 2, S5_STATE, w)
    zf = jnp.zeros_like(fg_re[:, 0])
    f = jnp.concatenate([
        jnp.concatenate([fg_re[:, 0], zf], axis=-1),
        jnp.concatenate([zf, fg_re[:, 1]], axis=-1),
        jnp.concatenate([fg_im[:, 0], zf], axis=-1),
        jnp.concatenate([zf, fg_im[:, 1]], axis=-1)], axis=1).astype(BF16)
    pw_re = pw_re.reshape(2, 1, (g // 2) * S5_STATE)
    pw_im = pw_im.reshape(2, 1, (g // 2) * S5_STATE)
    d_tile = jnp.tile(s5_d.reshape(g, 1, S5_GROUP), (1, 1, q))
    return m, e, f, pw_re, pw_im, d_tile


def _merge_mlp_kernel(x_ref, ya_ref, yb_ref, gate_ref, gw_ref, gb_ref, wa_ref, wb_ref, wo_ref,
                      g2_ref, wi_ref, wo2_ref, g3_ref, o_ref):
    yb = yb_ref[...]
    glu = jnp.dot(yb, gw_ref[...], preferred_element_type=F32) + gb_ref[...]
    ybg = (yb.astype(F32) * jax.nn.sigmoid(glu)).astype(BF16)
    pa = jnp.dot(ya_ref[...], wa_ref[...], preferred_element_type=F32)
    pb = jnp.dot(ybg, wb_ref[...], preferred_element_type=F32)
    merged = (gate_ref[:, :D_MODEL].astype(F32) * pa
              + gate_ref[:, D_MODEL:].astype(F32) * pb).astype(BF16)
    x1 = x_ref[...] + jnp.dot(merged, wo_ref[...], preferred_element_type=F32)
    h2 = (_rms(x1) * g2_ref[...]).astype(BF16)
    acc = x1
    for k in range(D_FF // D_MODEL):
        cols = slice(k * D_MODEL, (k + 1) * D_MODEL)
        hid = jnp.dot(h2, wi_ref[:, cols], preferred_element_type=F32)
        hid = jnp.square(jnp.maximum(hid, 0.0)).astype(BF16)
        acc = acc + jnp.dot(hid, wo2_ref[cols, :], preferred_element_type=F32)
    o_ref[...] = _rms(acc) * g3_ref[...]


def _merge_mlp(x2, ya, yb, gates, glu_w, glu_b, w_branch, w_out, g_mlp, w_mlp_in, w_mlp_out,
               g_final):
    t = x2.shape[0]
    tm = ROW_TILE
    wa = w_branch[:SSD_D_INNER].astype(BF16)
    wb = w_branch[SSD_D_INNER:].astype(BF16)
    row = lambda w: pl.BlockSpec((tm, w), lambda i: (i, 0))
    vec = lambda v: v.reshape(1, -1)
    return pl.pallas_call(
        _merge_mlp_kernel,
        grid=(t // tm,),
        in_specs=[row(D_MODEL), row(SSD_D_INNER), row(S5_WIDTH), row(N_BRANCHES * D_MODEL),
                  _const_spec((S5_WIDTH, S5_WIDTH)), _const_spec((1, S5_WIDTH)),
                  _const_spec(wa.shape), _const_spec(wb.shape), _const_spec((D_MODEL, D_MODEL)),
                  _const_spec((1, D_MODEL)), _const_spec((D_MODEL, D_FF)),
                  _const_spec((D_FF, D_MODEL)), _const_spec((1, D_MODEL))],
        out_specs=row(D_MODEL),
        out_shape=jax.ShapeDtypeStruct((t, D_MODEL), F32),
        compiler_params=pltpu.CompilerParams(dimension_semantics=("arbitrary",),
                                             vmem_limit_bytes=V7X_VMEM_LIMIT_BYTES),
        name="merge_mlp",
    )(x2, ya, yb, gates, glu_w.astype(BF16), vec(glu_b), wa, wb, w_out.astype(BF16),
      vec(g_mlp), w_mlp_in.astype(BF16), w_mlp_out.astype(BF16), vec(g_final))


def kernel(x, norm_mix_g, w_in, conv_w, conv_b, dt_bias, a_log, d_ssd, ssd_norm_g, s5_a_re, s5_a_im, s5_log_dt, s5_b_re, s5_b_im, s5_c_re, s5_c_im, s5_d, s5_glu_w, s5_glu_b, w_branch, w_out, norm_mlp_g, w_mlp_in, w_mlp_out, norm_final_g):
    bsz, seqlen, _ = x.shape
    t = bsz * seqlen
    assert seqlen % SSD_ROW_TILE == 0 and t % ROW_TILE == 0 and seqlen % S5_BLOCK == 0
    x2 = x.reshape(t, D_MODEL)

    z, xbc, dt, dtt, u5, gates = _in_proj(x2, norm_mix_g, w_in)
    ya = _ssd(xbc, z, dt, dtt, conv_w, conv_b, dt_bias, a_log, d_ssd, ssd_norm_g, bsz, seqlen)

    tables = _s5_tables(*_s5_prep(s5_a_re, s5_a_im, s5_log_dt, s5_b_re, s5_b_im,
                                  s5_c_re, s5_c_im), s5_d)
    nb = t // S5_BLOCK
    ug = jnp.transpose(u5.reshape(nb, S5_BLOCK, S5_N_GROUPS, S5_GROUP),
                       (2, 0, 1, 3)).reshape(S5_N_GROUPS, nb, S5_BLOCK_W)
    yg = _s5(ug, *tables, bsz)
    yb = jnp.transpose(yg.reshape(S5_N_GROUPS, nb, S5_BLOCK, S5_GROUP),
                       (1, 2, 0, 3)).reshape(t, S5_WIDTH)

    out = _merge_mlp(x2, ya, yb, gates, s5_glu_w, s5_glu_b, w_branch, w_out, norm_mlp_g,
                     w_mlp_in, w_mlp_out, norm_final_g)
    return out.reshape(bsz, seqlen, D_MODEL)
```

```python
import functools

import jax
import jax.numpy as jnp
from jax import lax
from jax.experimental import pallas as pl
from jax.experimental.pallas import tpu as pltpu

F32 = jnp.float32
BF16 = jnp.bfloat16

D_MODEL = 1024
SSD_D_INNER = D_MODEL
SSD_HEADDIM = 64
SSD_N_HEADS = SSD_D_INNER // SSD_HEADDIM
SSD_N_GROUPS = 4
SSD_HPG = SSD_N_HEADS // SSD_N_GROUPS
SSD_D_STATE = 128
SSD_CONV = 4
SSD_CHUNK = 128
SSD_GROUP_W = SSD_D_INNER // SSD_N_GROUPS
SSD_BC_W = SSD_N_GROUPS * SSD_D_STATE
SSD_CONV_DIM = SSD_D_INNER + 2 * SSD_BC_W
S5_WIDTH = D_MODEL // 2
S5_GROUP = 16
S5_N_GROUPS = S5_WIDTH // S5_GROUP
S5_STATE = 64
S5_BLOCK = 16
S5_BLOCK_W = S5_BLOCK * S5_GROUP
D_FF = 4 * D_MODEL
N_BRANCHES = 2
OFF_Z = 0
OFF_XBC = OFF_Z + SSD_D_INNER
OFF_DT = OFF_XBC + SSD_CONV_DIM
OFF_U = OFF_DT + SSD_N_HEADS
OFF_G = OFF_U + S5_WIDTH
EPS = 1e-6

V7X_LANES = 128
V7X_SUBLANES = 8
V7X_VMEM_LIMIT_BYTES = 56 * 1024 * 1024

ROW_TILE = 512
SSD_ROW_TILE = 512
S5_BLOCKS_PER_STEP = 128


def _const_spec(shape):
    zeros = (0,) * len(shape)
    return pl.BlockSpec(shape, lambda *_: zeros, pipeline_mode=pl.Buffered(1))


def _rms(x):
    return x * lax.rsqrt(jnp.mean(x * x, axis=-1, keepdims=True) + EPS)


def _split3(x):
    hi = x.astype(BF16)
    r1 = x - hi.astype(F32)
    mid = r1.astype(BF16)
    lo = (r1 - mid.astype(F32)).astype(BF16)
    return hi, mid, lo


def _softplus(x):
    return jnp.maximum(x, 0.0) + jnp.log(1.0 + jnp.exp(-jnp.abs(x)))


def _s5_prep_kernel(are_ref, aim_ref, ldt_ref, bre_ref, bim_ref, cre_ref, cim_ref,
                    taps_ref, ere_ref, eim_ref, fre_ref, fim_ref, pwre_ref, pwim_ref):
    a_re = are_ref[...]
    a_im = aim_ref[...]
    dt = jnp.exp(ldt_ref[...])
    mag = jnp.exp(a_re * dt)
    ab_re = mag * jnp.cos(a_im * dt)
    ab_im = mag * jnp.sin(a_im * dt)
    den = a_re * a_re + a_im * a_im
    nr = ab_re - 1.0
    ni = ab_im
    coef_re = (nr * a_re + ni * a_im) / den
    coef_im = (ni * a_re - nr * a_im) / den
    b_re = bre_ref[...]
    b_im = bim_ref[...]
    bb_re = coef_re * b_re - coef_im * b_im
    bb_im = coef_re * b_im + coef_im * b_re
    c_re = cre_ref[...]
    c_im = cim_ref[...]

    p_re = jnp.ones_like(ab_re)
    p_im = jnp.zeros_like(ab_re)
    ca_re = []
    ca_im = []
    for tau in range(S5_BLOCK + 1):
        ca_re.append(c_re * p_re - c_im * p_im)
        ca_im.append(c_re * p_im + c_im * p_re)
        if tau < S5_BLOCK:
            s = S5_BLOCK - 1 - tau
            ere_ref[s] = p_re * bb_re - p_im * bb_im
            eim_ref[s] = p_re * bb_im + p_im * bb_re
        if tau >= 1:
            fre_ref[tau - 1] = ca_re[tau]
            fim_ref[tau - 1] = -ca_im[tau]
        if tau == S5_BLOCK:
            pwre_ref[...] = p_re
            pwim_ref[...] = p_im
        p_re, p_im = (p_re * ab_re - p_im * ab_im, p_re * ab_im + p_im * ab_re)

    car = jnp.concatenate(ca_re[:S5_BLOCK], axis=1)
    cai = jnp.concatenate(ca_im[:S5_BLOCK], axis=1)
    dn = (((2,), (2,)), ((0,), (0,)))
    taps = (lax.dot_general(bb_re, car, dn, precision=lax.Precision.HIGHEST,
                            preferred_element_type=F32)
            - lax.dot_general(bb_im, cai, dn, precision=lax.Precision.HIGHEST,
                              preferred_element_type=F32))
    taps_ref[...] = taps


def _s5_prep(a_re, a_im, log_dt, b_re, b_im, c_re, c_im):
    g = S5_N_GROUPS
    bt_re = jnp.transpose(b_re, (0, 2, 1))
    bt_im = jnp.transpose(b_im, (0, 2, 1))
    tab = jax.ShapeDtypeStruct((S5_BLOCK, g, S5_GROUP, S5_STATE), F32)
    return pl.pallas_call(
        _s5_prep_kernel,
        out_shape=(jax.ShapeDtypeStruct((g, S5_GROUP, S5_BLOCK_W), F32), tab, tab, tab, tab,
                   jax.ShapeDtypeStruct((g, 1, S5_STATE), F32),
                   jax.ShapeDtypeStruct((g, 1, S5_STATE), F32)),
        name="s5_prep",
    )(a_re.reshape(g, 1, S5_STATE), a_im.reshape(g, 1, S5_STATE), log_dt.reshape(g, 1, 1),
      bt_re, bt_im, c_re, c_im)


def _in_proj_kernel(x_ref, g_ref, wz_ref, wxbc_ref, wdt_ref, wdtt_ref, wu_ref, wg_ref,
                    cw_ref, cb_ref, z_ref, xact_ref, dt_ref, dtt_ref, u_ref, gate_ref, xpad,
                    *, tiles_per_seq):
    tm = x_ref.shape[0]
    halo = V7X_SUBLANES
    first = (pl.program_id(0) % tiles_per_seq) == 0

    @pl.when(first)
    def _():
        xpad[0:halo, :] = jnp.zeros((halo, SSD_CONV_DIM), F32)

    @pl.when(jnp.logical_not(first))
    def _():
        xpad[0:halo, :] = xpad[tm:tm + halo, :]

    h = (_rms(x_ref[...]) * g_ref[...]).astype(BF16)
    xpad[halo:halo + tm, :] = jnp.dot(h, wxbc_ref[...], preferred_element_type=F32)
    z_ref[...] = jnp.dot(h, wz_ref[...], preferred_element_type=F32).astype(BF16)
    u = jnp.dot(h, wu_ref[...], preferred_element_type=F32)
    for j in range(u_ref.shape[0]):
        u_ref[j] = u[:, j * V7X_LANES:(j + 1) * V7X_LANES]
    gate_ref[...] = jax.nn.sigmoid(
        jnp.dot(h, wg_ref[...], preferred_element_type=F32)).astype(BF16)
    dt = jnp.dot(h, wdt_ref[...], preferred_element_type=F32)
    dt_ref[...] = dt[:, :SSD_N_HEADS]
    dtt = lax.dot_general(wdtt_ref[...], h, (((1,), (1,)), ((), ())),
                          preferred_element_type=F32)
    for c in range(dtt_ref.shape[0]):
        dtt_ref[c] = dtt[:, c * SSD_CHUNK:(c + 1) * SSD_CHUNK]

    for c in range(tm // SSD_CHUNK):
        base = c * SSD_CHUNK + halo - (SSD_CONV - 1)
        acc = cb_ref[...]
        for k in range(SSD_CONV):
            acc = acc + cw_ref[k:k + 1, :] * xpad[base + k:base + k + SSD_CHUNK, :]
        xact_ref[c * SSD_CHUNK:(c + 1) * SSD_CHUNK, :] = (acc * jax.nn.sigmoid(acc)).astype(BF16)


def _in_proj(x2, g_mix, w_in, conv_w, conv_b, seqlen):
    t = x2.shape[0]
    tm = ROW_TILE
    wz = w_in[:, OFF_Z:OFF_XBC].astype(BF16)
    wxbc = w_in[:, OFF_XBC:OFF_DT].astype(BF16)
    wdt = w_in[:, OFF_DT:OFF_U]
    wdt_pad = jnp.pad(wdt, ((0, 0), (0, V7X_LANES - SSD_N_HEADS))).astype(BF16)
    wdtt = wdt.T.astype(BF16)
    wu = w_in[:, OFF_U:OFF_G].astype(BF16)
    wg = w_in[:, OFF_G:].astype(BF16)
    n_slab = S5_WIDTH // V7X_LANES
    row = lambda w: pl.BlockSpec((tm, w), lambda i: (i, 0))
    return pl.pallas_call(
        functools.partial(_in_proj_kernel, tiles_per_seq=seqlen // tm),
        grid=(t // tm,),
        in_specs=[row(D_MODEL), _const_spec((1, D_MODEL)), _const_spec(wz.shape),
                  _const_spec(wxbc.shape), _const_spec(wdt_pad.shape), _const_spec(wdtt.shape),
                  _const_spec(wu.shape), _const_spec(wg.shape),
                  _const_spec((SSD_CONV, SSD_CONV_DIM)), _const_spec((1, SSD_CONV_DIM))],
        out_specs=[row(SSD_D_INNER), row(SSD_CONV_DIM), row(SSD_N_HEADS),
                   pl.BlockSpec((tm // SSD_CHUNK, SSD_N_HEADS, SSD_CHUNK), lambda i: (i, 0, 0)),
                   pl.BlockSpec((n_slab, tm, V7X_LANES), lambda i: (0, i, 0)),
                   row(N_BRANCHES * D_MODEL)],
        out_shape=(jax.ShapeDtypeStruct((t, SSD_D_INNER), BF16),
                   jax.ShapeDtypeStruct((t, SSD_CONV_DIM), BF16),
                   jax.ShapeDtypeStruct((t, SSD_N_HEADS), F32),
                   jax.ShapeDtypeStruct((t // SSD_CHUNK, SSD_N_HEADS, SSD_CHUNK), F32),
                   jax.ShapeDtypeStruct((n_slab, t, V7X_LANES), F32),
                   jax.ShapeDtypeStruct((t, N_BRANCHES * D_MODEL), BF16)),
        scratch_shapes=[pltpu.VMEM((tm + V7X_SUBLANES, SSD_CONV_DIM), F32)],
        compiler_params=pltpu.CompilerParams(dimension_semantics=("arbitrary",),
                                             vmem_limit_bytes=V7X_VMEM_LIMIT_BYTES),
        name="in_proj",
    )(x2, g_mix.reshape(1, D_MODEL), wz, wxbc, wdt_pad, wdtt, wu, wg, conv_w,
      conv_b.reshape(1, -1))


def _ssd_kernel(xact_ref, z_ref, dt_ref, dtt_ref, dtbc_ref, dtbr_ref, ac_ref, ar_ref,
                dcol_ref, ng_ref, y_ref, st_s):
    tm = xact_ref.shape[0]
    nch = tm // SSD_CHUNK

    @pl.when(pl.program_id(1) == 0)
    def _():
        st_s[...] = jnp.zeros_like(st_s)

    q = SSD_CHUNK
    row_i = lax.broadcasted_iota(jnp.int32, (q, q), 0)
    col_i = lax.broadcasted_iota(jnp.int32, (q, q), 1)
    causal = row_i >= col_i
    tri = causal.astype(BF16)
    tri_t = (row_i <= col_i).astype(BF16)
    head_of_lane = lax.broadcasted_iota(jnp.int32, (q, SSD_GROUP_W), 1) // SSD_HEADDIM
    widen = (lax.broadcasted_iota(jnp.int32, (SSD_N_HEADS, SSD_D_INNER), 1) // SSD_HEADDIM
             == lax.broadcasted_iota(jnp.int32, (SSD_N_HEADS, SSD_D_INNER), 0)).astype(BF16)
    a_c = -jnp.exp(ac_ref[...])
    a_r = -jnp.exp(ar_ref[...])

    def chunk(c, carry):
        r0 = pl.multiple_of(c * q, q)
        dt_c = _softplus(dt_ref[pl.ds(r0, q), :] + dtbc_ref[...])
        dt_r = _softplus(dtt_ref[c] + dtbr_ref[...])
        la_c = dt_c * a_c
        la_r = dt_r * a_r
        cum_c = sum(jnp.dot(tri, p, preferred_element_type=F32) for p in _split3(la_c))
        cum_r = sum(jnp.dot(p, tri_t, preferred_element_type=F32) for p in _split3(la_r))
        to_end = dt_c * jnp.exp(cum_c[q - 1:q, :] - cum_c)
        dt_x = jnp.dot(dt_c.astype(BF16), widen, preferred_element_type=F32)
        te_x = jnp.dot(to_end.astype(BF16), widen, preferred_element_type=F32)
        ec_x = jnp.dot(jnp.exp(cum_c).astype(BF16), widen, preferred_element_type=F32)

        for g in range(SSD_N_GROUPS):
            b0 = SSD_D_INNER + g * SSD_D_STATE
            c0 = SSD_D_INNER + SSD_BC_W + g * SSD_D_STATE
            bg = xact_ref[pl.ds(r0, q), b0:b0 + SSD_D_STATE]
            cg = xact_ref[pl.ds(r0, q), c0:c0 + SSD_D_STATE]
            cb = lax.dot_general(cg, bg, (((1,), (1,)), ((), ())),
                                 preferred_element_type=F32)
            ws = []
            for r in range(SSD_HPG):
                h = g * SSD_HPG + r
                cc = jnp.broadcast_to(cum_c[:, h:h + 1], (q, q))
                cr = jnp.broadcast_to(cum_r[h:h + 1, :], (q, q))
                lm = jnp.exp(jnp.where(causal, cc - cr, -jnp.inf))
                ws.append((cb * lm).astype(BF16))
            wcat = jnp.concatenate(ws, axis=1)
            cols = slice(g * SSD_GROUP_W, (g + 1) * SSD_GROUP_W)
            xs_g = xact_ref[pl.ds(r0, q), cols].astype(F32)
            xdt_b = (xs_g * dt_x[:, cols]).astype(BF16)
            xdtd = (xs_g * te_x[:, cols]).astype(BF16)
            xbd = jnp.concatenate(
                [jnp.where(head_of_lane == r, xdt_b, jnp.zeros_like(xdt_b))
                 for r in range(SSD_HPG)], axis=0)
            y = jnp.dot(wcat, xbd, preferred_element_type=F32)
            prev = st_s[:, cols]
            y = y + jnp.dot(cg, prev.astype(BF16), preferred_element_type=F32) * ec_x[:, cols]
            st_s[:, cols] = prev * ec_x[q - 1:q, cols] + lax.dot_general(
                bg, xdtd, (((0,), (0,)), ((), ())), preferred_element_type=F32)
            y = y + dcol_ref[:, cols] * xs_g
            zg = z_ref[pl.ds(r0, q), cols].astype(F32)
            y = y * (zg * jax.nn.sigmoid(zg))
            y_ref[pl.ds(r0, q), cols] = (_rms(y) * ng_ref[:, cols]).astype(BF16)
        return carry

    lax.fori_loop(0, nch, chunk, 0)


def _ssd(xact, z, dt, dtt, dt_bias, a_log, d_ssd, ssd_norm_g, bsz, seqlen):
    tm = SSD_ROW_TILE
    nt = seqlen // tm
    nch = tm // SSD_CHUNK
    h = SSD_N_HEADS
    row = lambda w: pl.BlockSpec((tm, w), lambda b, i: (b * nt + i, 0))
    return pl.pallas_call(
        _ssd_kernel,
        grid=(bsz, nt),
        in_specs=[row(SSD_CONV_DIM), row(SSD_D_INNER), row(h),
                  pl.BlockSpec((nch, h, SSD_CHUNK), lambda b, i: (b * nt + i, 0, 0)),
                  _const_spec((1, h)), _const_spec((h, 1)), _const_spec((1, h)),
                  _const_spec((h, 1)), _const_spec((1, SSD_D_INNER)),
                  _const_spec((1, SSD_D_INNER))],
        out_specs=row(SSD_D_INNER),
        out_shape=jax.ShapeDtypeStruct((bsz * seqlen, SSD_D_INNER), BF16),
        scratch_shapes=[pltpu.VMEM((SSD_D_STATE, SSD_D_INNER), F32)],
        compiler_params=pltpu.CompilerParams(dimension_semantics=("arbitrary", "arbitrary"),
                                             vmem_limit_bytes=V7X_VMEM_LIMIT_BYTES),
        name="ssd",
    )(xact, z, dt, dtt, dt_bias.reshape(1, h), dt_bias.reshape(h, 1), a_log.reshape(1, h),
      a_log.reshape(h, 1), jnp.repeat(d_ssd, SSD_HEADDIM).reshape(1, SSD_D_INNER),
      ssd_norm_g.reshape(1, -1))


def _s5_kernel(u_ref, mt_ref, et_ref, ft_ref, pre_ref, pim_ref, d_ref, y_ref,
               ut_s, yt_s, sre, sim, cre, cim):
    n_slab = u_ref.shape[0]
    nblk = u_ref.shape[1] // S5_BLOCK
    npair = S5_N_GROUPS // 2
    gps = V7X_LANES // S5_GROUP
    hw = V7X_LANES

    @pl.when(pl.program_id(1) == 0)
    def _():
        cre[...] = jnp.zeros_like(cre)
        cim[...] = jnp.zeros_like(cim)

    for s in range(S5_BLOCK):
        for j in range(n_slab):
            xt = u_ref[j, pl.ds(s, nblk, stride=S5_BLOCK), :].T
            for gl in range(gps):
                ut_s[j * gps + gl, s * S5_GROUP:(s + 1) * S5_GROUP, :] = (
                    xt[gl * S5_GROUP:(gl + 1) * S5_GROUP, :])

    for pr in range(npair):
        ucat = jnp.concatenate([ut_s[2 * pr], ut_s[2 * pr + 1]], axis=0).astype(BF16)
        sl = jnp.dot(et_ref[pr], ucat, preferred_element_type=F32).T
        sre[pr] = sl[:, :hw]
        sim[pr] = sl[:, hw:]

    a_re = pre_ref[...]
    a_im = pim_ref[...]

    def step(r, carry):
        c_re, c_im = carry
        l_re = sre[:, pl.ds(r, 1), :]
        l_im = sim[:, pl.ds(r, 1), :]
        sre[:, pl.ds(r, 1), :] = c_re
        sim[:, pl.ds(r, 1), :] = c_im
        return (a_re * c_re - a_im * c_im + l_re, a_re * c_im + a_im * c_re + l_im)

    c_re, c_im = lax.fori_loop(0, nblk, step, (cre[...], cim[...]))
    cre[...] = c_re
    cim[...] = c_im

    for pr in range(npair):
        sp = jnp.concatenate([sre[pr], sim[pr]], axis=1).astype(BF16)
        cross = lax.dot_general(ft_ref[pr], sp, (((1,), (1,)), ((), ())),
                                preferred_element_type=F32)
        for j in range(2):
            gi = 2 * pr + j
            ug = ut_s[gi]
            y = (jnp.dot(mt_ref[gi], ug.astype(BF16), preferred_element_type=F32)
                 + cross[j * S5_BLOCK_W:(j + 1) * S5_BLOCK_W, :]
                 + d_ref[gi] * ug)
            yt_s[gi] = jax.nn.gelu(y)

    for t in range(S5_BLOCK):
        for j in range(n_slab):
            zt = jnp.concatenate(
                [yt_s[j * gps + gl, t * S5_GROUP:(t + 1) * S5_GROUP, :] for gl in range(gps)],
                axis=0)
            y_ref[j, pl.ds(t, nblk, stride=S5_BLOCK), :] = zt.T


def _s5(u_slab, mt, et, ft, pw_re, pw_im, d_col, bsz):
    n_slab, t, _ = u_slab.shape
    nblk = S5_BLOCKS_PER_STEP
    rows = nblk * S5_BLOCK
    parts = t // bsz // rows
    g, w = S5_N_GROUPS, S5_BLOCK_W
    io = pl.BlockSpec((n_slab, rows, V7X_LANES), lambda b, p: (0, b * parts + p, 0))
    state = pltpu.VMEM((g // 2, nblk, V7X_LANES), F32)
    carry = pltpu.VMEM((g // 2, 1, V7X_LANES), F32)
    return pl.pallas_call(
        _s5_kernel,
        grid=(bsz, parts),
        in_specs=[io, _const_spec(mt.shape), _const_spec(et.shape), _const_spec(ft.shape),
                  _const_spec(pw_re.shape), _const_spec(pw_im.shape), _const_spec(d_col.shape)],
        out_specs=io,
        out_shape=jax.ShapeDtypeStruct(u_slab.shape, F32),
        scratch_shapes=[pltpu.VMEM((g, w, nblk), F32), pltpu.VMEM((g, w, nblk), F32),
                        state, state, carry, carry],
        compiler_params=pltpu.CompilerParams(dimension_semantics=("arbitrary", "arbitrary"),
                                             vmem_limit_bytes=V7X_VMEM_LIMIT_BYTES),
        name="s5",
    )(u_slab, mt, et, ft, pw_re, pw_im, d_col)


def _s5_tables(taps, e_re, e_im, f_re, f_im, pw_re, pw_im, s5_d):
    g, q, w = S5_N_GROUPS, S5_BLOCK, S5_BLOCK_W
    m = jnp.stack([jnp.pad(taps[:, :, :w - S5_GROUP * s], ((0, 0), (0, 0), (S5_GROUP * s, 0)))
                   for s in range(q)], axis=1).reshape(g, w, w)
    eg_re = jnp.transpose(e_re, (1, 0, 2, 3)).reshape(g // 2, 2, w, S5_STATE)
    eg_im = jnp.transpose(e_im, (1, 0, 2, 3)).reshape(g // 2, 2, w, S5_STATE)
    zero = jnp.zeros_like(eg_re[:, 0])
    e_top = jnp.concatenate([eg_re[:, 0], zero, eg_im[:, 0], zero], axis=-1)
    e_bot = jnp.concatenate([zero, eg_re[:, 1], zero, eg_im[:, 1]], axis=-1)
    e = jnp.concatenate([e_top, e_bot], axis=1)
    fg_re = jnp.transpose(f_re, (1, 3, 0, 2)).reshape(g // 2, 2, S5_STATE, w)
    fg_im = jnp.transpose(f_im, (1, 3, 0, 2)).reshape(g // 2, 2, S5_STATE, w)
    zf = jnp.zeros_like(fg_re[:, 0])
    f = jnp.concatenate([
        jnp.concatenate([fg_re[:, 0], zf], axis=-1),
        jnp.concatenate([zf, fg_re[:, 1]], axis=-1),
        jnp.concatenate([fg_im[:, 0], zf], axis=-1),
        jnp.concatenate([zf, fg_im[:, 1]], axis=-1)], axis=1)
    tr = lambda a: jnp.swapaxes(a, 1, 2).astype(BF16)
    pw_re = pw_re.reshape(g // 2, 1, 2 * S5_STATE)
    pw_im = pw_im.reshape(g // 2, 1, 2 * S5_STATE)
    d_col = jnp.tile(s5_d.reshape(g, 1, S5_GROUP), (1, q, 1)).reshape(g, w, 1)
    return tr(m), tr(e), tr(f), pw_re, pw_im, d_col


def _merge_mlp_kernel(x_ref, ya_ref, yb_ref, gate_ref, gw_ref, gb_ref, wa_ref, wb_ref, wo_ref,
                      g2_ref, wi_ref, wo2_ref, g3_ref, o_ref):
    ybf = jnp.concatenate([yb_ref[j] for j in range(yb_ref.shape[0])], axis=1)
    yb = ybf.astype(BF16)
    glu = jnp.dot(yb, gw_ref[...], preferred_element_type=F32) + gb_ref[...]
    ybg = (ybf * jax.nn.sigmoid(glu)).astype(BF16)
    pa = jnp.dot(ya_ref[...], wa_ref[...], preferred_element_type=F32)
    pb = jnp.dot(ybg, wb_ref[...], preferred_element_type=F32)
    merged = (gate_ref[:, :D_MODEL].astype(F32) * pa
              + gate_ref[:, D_MODEL:].astype(F32) * pb).astype(BF16)
    x1 = x_ref[...] + jnp.dot(merged, wo_ref[...], preferred_element_type=F32)
    h2 = (_rms(x1) * g2_ref[...]).astype(BF16)
    acc = x1
    for k in range(D_FF // D_MODEL):
        cols = slice(k * D_MODEL, (k + 1) * D_MODEL)
        hid = jnp.dot(h2, wi_ref[:, cols], preferred_element_type=F32)
        hid = jnp.square(jnp.maximum(hid, 0.0)).astype(BF16)
        acc = acc + jnp.dot(hid, wo2_ref[cols, :], preferred_element_type=F32)
    o_ref[...] = _rms(acc) * g3_ref[...]


def _merge_mlp(x2, ya, yb_slab, gates, glu_w, glu_b, w_branch, w_out, g_mlp, w_mlp_in,
               w_mlp_out, g_final):
    t = x2.shape[0]
    tm = ROW_TILE
    wa = w_branch[:SSD_D_INNER].astype(BF16)
    wb = w_branch[SSD_D_INNER:].astype(BF16)
    row = lambda w: pl.BlockSpec((tm, w), lambda i: (i, 0))
    vec = lambda v: v.reshape(1, -1)
    return pl.pallas_call(
        _merge_mlp_kernel,
        grid=(t // tm,),
        in_specs=[row(D_MODEL), row(SSD_D_INNER),
                  pl.BlockSpec((yb_slab.shape[0], tm, V7X_LANES), lambda i: (0, i, 0)),
                  row(N_BRANCHES * D_MODEL),
                  _const_spec((S5_WIDTH, S5_WIDTH)), _const_spec((1, S5_WIDTH)),
                  _const_spec(wa.shape), _const_spec(wb.shape), _const_spec((D_MODEL, D_MODEL)),
                  _const_spec((1, D_MODEL)), _const_spec((D_MODEL, D_FF)),
                  _const_spec((D_FF, D_MODEL)), _const_spec((1, D_MODEL))],
        out_specs=row(D_MODEL),
        out_shape=jax.ShapeDtypeStruct((t, D_MODEL), F32),
        compiler_params=pltpu.CompilerParams(dimension_semantics=("arbitrary",),
                                             vmem_limit_bytes=V7X_VMEM_LIMIT_BYTES),
        name="merge_mlp",
    )(x2, ya, yb_slab, gates, glu_w.astype(BF16), vec(glu_b), wa, wb, w_out.astype(BF16),
      vec(g_mlp), w_mlp_in.astype(BF16), w_mlp_out.astype(BF16), vec(g_final))


def kernel(x, norm_mix_g, w_in, conv_w, conv_b, dt_bias, a_log, d_ssd, ssd_norm_g, s5_a_re, s5_a_im, s5_log_dt, s5_b_re, s5_b_im, s5_c_re, s5_c_im, s5_d, s5_glu_w, s5_glu_b, w_branch, w_out, norm_mlp_g, w_mlp_in, w_mlp_out, norm_final_g):
    bsz, seqlen, _ = x.shape
    t = bsz * seqlen
    assert seqlen % SSD_ROW_TILE == 0 and seqlen % ROW_TILE == 0
    assert seqlen % (S5_BLOCKS_PER_STEP * S5_BLOCK) == 0
    x2 = x.reshape(t, D_MODEL)

    z, xact, dt, dtt, u_slab, gates = _in_proj(x2, norm_mix_g, w_in, conv_w, conv_b, seqlen)
    ya = _ssd(xact, z, dt, dtt, dt_bias, a_log, d_ssd, ssd_norm_g, bsz, seqlen)
    tables = _s5_tables(*_s5_prep(s5_a_re, s5_a_im, s5_log_dt, s5_b_re, s5_b_im,
                                  s5_c_re, s5_c_im), s5_d)
    yb_slab = _s5(u_slab, *tables, bsz)
    out = _merge_mlp(x2, ya, yb_slab, gates, s5_glu_w, s5_glu_b, w_branch, w_out, norm_mlp_g,
                     w_mlp_in, w_mlp_out, norm_final_g)
    return out.reshape(bsz, seqlen, D_MODEL)
```

```python
import jax
import jax.numpy as jnp
from jax import lax
from jax.experimental import pallas as pl
from jax.experimental.pallas import tpu as pltpu

F32 = jnp.float32
BF16 = jnp.bfloat16

D_MODEL = 1024
SSD_D_INNER = D_MODEL
SSD_HEADDIM = 64
SSD_N_HEADS = SSD_D_INNER // SSD_HEADDIM
SSD_N_GROUPS = 4
SSD_HPG = SSD_N_HEADS // SSD_N_GROUPS
SSD_D_STATE = 128
SSD_CONV = 4
SSD_CHUNK = 128
SSD_GROUP_W = SSD_D_INNER // SSD_N_GROUPS
SSD_BC_W = SSD_N_GROUPS * SSD_D_STATE
SSD_CONV_DIM = SSD_D_INNER + 2 * SSD_BC_W
S5_WIDTH = D_MODEL // 2
S5_GROUP = 16
S5_N_GROUPS = S5_WIDTH // S5_GROUP
S5_STATE = 64
S5_BLOCK = 16
S5_BLOCK_W = S5_BLOCK * S5_GROUP
D_FF = 4 * D_MODEL
N_BRANCHES = 2
OFF_Z = 0
OFF_XBC = OFF_Z + SSD_D_INNER
OFF_DT = OFF_XBC + SSD_CONV_DIM
OFF_U = OFF_DT + SSD_N_HEADS
OFF_G = OFF_U + S5_WIDTH
EPS = 1e-6

V7X_LANES = 128
V7X_SUBLANES = 8
V7X_VMEM_LIMIT_BYTES = 56 * 1024 * 1024

ROW_TILE = 512
SSD_ROW_TILE = 512
S5_BLOCKS_PER_STEP = 128


def _const_spec(shape):
    zeros = (0,) * len(shape)
    return pl.BlockSpec(shape, lambda *_: zeros, pipeline_mode=pl.Buffered(1))


def _rms(x):
    return x * lax.rsqrt(jnp.mean(x * x, axis=-1, keepdims=True) + EPS)


def _split3(x):
    hi = x.astype(BF16)
    r1 = x - hi.astype(F32)
    mid = r1.astype(BF16)
    lo = (r1 - mid.astype(F32)).astype(BF16)
    return hi, mid, lo


def _softplus(x):
    return jnp.maximum(x, 0.0) + jnp.log(1.0 + jnp.exp(-jnp.abs(x)))


def _silu(x):
    h = 0.5 * x
    return h + h * jnp.tanh(h)


def _s5_prep_kernel(are_ref, aim_ref, ldt_ref, bre_ref, bim_ref, cre_ref, cim_ref,
                    taps_ref, ere_ref, eim_ref, fre_ref, fim_ref, pwre_ref, pwim_ref):
    a_re = are_ref[...]
    a_im = aim_ref[...]
    dt = jnp.exp(ldt_ref[...])
    mag = jnp.exp(a_re * dt)
    ab_re = mag * jnp.cos(a_im * dt)
    ab_im = mag * jnp.sin(a_im * dt)
    den = a_re * a_re + a_im * a_im
    nr = ab_re - 1.0
    ni = ab_im
    coef_re = (nr * a_re + ni * a_im) / den
    coef_im = (ni * a_re - nr * a_im) / den
    b_re = bre_ref[...]
    b_im = bim_ref[...]
    bb_re = coef_re * b_re - coef_im * b_im
    bb_im = coef_re * b_im + coef_im * b_re
    c_re = cre_ref[...]
    c_im = cim_ref[...]

    p_re = jnp.ones_like(ab_re)
    p_im = jnp.zeros_like(ab_re)
    ca_re = []
    ca_im = []
    for tau in range(S5_BLOCK + 1):
        ca_re.append(c_re * p_re - c_im * p_im)
        ca_im.append(c_re * p_im + c_im * p_re)
        if tau < S5_BLOCK:
            s = S5_BLOCK - 1 - tau
            ere_ref[s] = p_re * bb_re - p_im * bb_im
            eim_ref[s] = p_re * bb_im + p_im * bb_re
        if tau >= 1:
            fre_ref[tau - 1] = ca_re[tau]
            fim_ref[tau - 1] = -ca_im[tau]
        if tau == S5_BLOCK:
            pwre_ref[...] = p_re
            pwim_ref[...] = p_im
        p_re, p_im = (p_re * ab_re - p_im * ab_im, p_re * ab_im + p_im * ab_re)

    car = jnp.concatenate(ca_re[:S5_BLOCK], axis=1)
    cai = jnp.concatenate(ca_im[:S5_BLOCK], axis=1)
    dn = (((2,), (2,)), ((0,), (0,)))
    taps = (lax.dot_general(bb_re, car, dn, precision=lax.Precision.HIGHEST,
                            preferred_element_type=F32)
            - lax.dot_general(bb_im, cai, dn, precision=lax.Precision.HIGHEST,
                              preferred_element_type=F32))
    taps_ref[...] = taps


def _s5_prep(a_re, a_im, log_dt, b_re, b_im, c_re, c_im):
    g = S5_N_GROUPS
    bt_re = jnp.transpose(b_re, (0, 2, 1))
    bt_im = jnp.transpose(b_im, (0, 2, 1))
    tab = jax.ShapeDtypeStruct((S5_BLOCK, g, S5_GROUP, S5_STATE), F32)
    return pl.pallas_call(
        _s5_prep_kernel,
        out_shape=(jax.ShapeDtypeStruct((g, S5_GROUP, S5_BLOCK_W), F32), tab, tab, tab, tab,
                   jax.ShapeDtypeStruct((g, 1, S5_STATE), F32),
                   jax.ShapeDtypeStruct((g, 1, S5_STATE), F32)),
        name="s5_prep",
    )(a_re.reshape(g, 1, S5_STATE), a_im.reshape(g, 1, S5_STATE), log_dt.reshape(g, 1, 1),
      bt_re, bt_im, c_re, c_im)


def _in_proj_kernel(x_ref, g_ref, wz_ref, wxbc_ref, wdt_ref, wdtt_ref, wu_ref, wg_ref,
                    z_ref, xbc_ref, dt_ref, dtt_ref, u_ref, gate_ref):
    h = (_rms(x_ref[...]) * g_ref[...]).astype(BF16)
    xbc_ref[...] = jnp.dot(h, wxbc_ref[...], preferred_element_type=F32).astype(BF16)
    z_ref[...] = jnp.dot(h, wz_ref[...], preferred_element_type=F32).astype(BF16)
    u = jnp.dot(h, wu_ref[...], preferred_element_type=F32)
    for j in range(u_ref.shape[0]):
        u_ref[j] = u[:, j * V7X_LANES:(j + 1) * V7X_LANES]
    gate_ref[...] = jax.nn.sigmoid(
        jnp.dot(h, wg_ref[...], preferred_element_type=F32)).astype(BF16)
    dt = jnp.dot(h, wdt_ref[...], preferred_element_type=F32)
    dt_ref[...] = dt[:, :SSD_N_HEADS]
    dtt = lax.dot_general(wdtt_ref[...], h, (((1,), (1,)), ((), ())),
                          preferred_element_type=F32)
    for c in range(dtt_ref.shape[0]):
        dtt_ref[c] = dtt[:, c * SSD_CHUNK:(c + 1) * SSD_CHUNK]


def _in_proj(x2, g_mix, w_in):
    t = x2.shape[0]
    tm = ROW_TILE
    wz = w_in[:, OFF_Z:OFF_XBC].astype(BF16)
    wxbc = w_in[:, OFF_XBC:OFF_DT].astype(BF16)
    wdt = w_in[:, OFF_DT:OFF_U]
    wdt_pad = jnp.pad(wdt, ((0, 0), (0, V7X_LANES - SSD_N_HEADS))).astype(BF16)
    wdtt = wdt.T.astype(BF16)
    wu = w_in[:, OFF_U:OFF_G].astype(BF16)
    wg = w_in[:, OFF_G:].astype(BF16)
    n_slab = S5_WIDTH // V7X_LANES
    row = lambda w: pl.BlockSpec((tm, w), lambda i: (i, 0))
    return pl.pallas_call(
        _in_proj_kernel,
        grid=(t // tm,),
        in_specs=[row(D_MODEL), _const_spec((1, D_MODEL)), _const_spec(wz.shape),
                  _const_spec(wxbc.shape), _const_spec(wdt_pad.shape), _const_spec(wdtt.shape),
                  _const_spec(wu.shape), _const_spec(wg.shape)],
        out_specs=[row(SSD_D_INNER), row(SSD_CONV_DIM), row(SSD_N_HEADS),
                   pl.BlockSpec((tm // SSD_CHUNK, SSD_N_HEADS, SSD_CHUNK), lambda i: (i, 0, 0)),
                   pl.BlockSpec((n_slab, tm, V7X_LANES), lambda i: (0, i, 0)),
                   row(N_BRANCHES * D_MODEL)],
        out_shape=(jax.ShapeDtypeStruct((t, SSD_D_INNER), BF16),
                   jax.ShapeDtypeStruct((t, SSD_CONV_DIM), BF16),
                   jax.ShapeDtypeStruct((t, SSD_N_HEADS), F32),
                   jax.ShapeDtypeStruct((t // SSD_CHUNK, SSD_N_HEADS, SSD_CHUNK), F32),
                   jax.ShapeDtypeStruct((n_slab, t, V7X_LANES), F32),
                   jax.ShapeDtypeStruct((t, N_BRANCHES * D_MODEL), BF16)),
        compiler_params=pltpu.CompilerParams(dimension_semantics=("arbitrary",),
                                             vmem_limit_bytes=V7X_VMEM_LIMIT_BYTES),
        name="in_proj",
    )(x2, g_mix.reshape(1, D_MODEL), wz, wxbc, wdt_pad, wdtt, wu, wg)


def _ssd_kernel(xbc_ref, z_ref, dt_ref, dtt_ref, cw_ref, cb_ref, dtbc_ref, dtbr_ref, ac_ref,
                ar_ref, dcol_ref, ng_ref, y_ref, xraw, st_s):
    tm = xbc_ref.shape[0]
    nch = tm // SSD_CHUNK
    q = SSD_CHUNK
    halo = V7X_SUBLANES
    first = pl.program_id(1) == 0

    @pl.when(first)
    def _():
        st_s[...] = jnp.zeros_like(st_s)
        xraw[0:halo, :] = jnp.zeros((halo, SSD_CONV_DIM), F32)

    @pl.when(jnp.logical_not(first))
    def _():
        xraw[0:halo, :] = xraw[tm:tm + halo, :]

    xraw[halo:halo + tm, :] = xbc_ref[...].astype(F32)

    def conv(r0, c0, width):
        ext = xraw[pl.ds(r0, halo + q), c0:c0 + width]
        acc = cb_ref[:, c0:c0 + width] + (cw_ref[SSD_CONV - 1:SSD_CONV, c0:c0 + width]
                                          * ext[halo:, :])
        for k in range(SSD_CONV - 1):
            acc = acc + (cw_ref[k:k + 1, c0:c0 + width]
                         * pltpu.roll(ext, SSD_CONV - 1 - k, 0)[halo:, :])
        return _silu(acc)

    row_i = lax.broadcasted_iota(jnp.int32, (q, q), 0)
    col_i = lax.broadcasted_iota(jnp.int32, (q, q), 1)
    causal = row_i >= col_i
    tri = causal.astype(BF16)
    tri_t = (row_i <= col_i).astype(BF16)
    head_of_lane = lax.broadcasted_iota(jnp.int32, (q, SSD_GROUP_W), 1) // SSD_HEADDIM
    widen = (lax.broadcasted_iota(jnp.int32, (SSD_N_HEADS, SSD_D_INNER), 1) // SSD_HEADDIM
             == lax.broadcasted_iota(jnp.int32, (SSD_N_HEADS, SSD_D_INNER), 0)).astype(BF16)
    a_c = -jnp.exp(ac_ref[...])
    a_r = -jnp.exp(ar_ref[...])

    def chunk(c, carry):
        r0 = pl.multiple_of(c * q, q)
        dt_c = _softplus(dt_ref[pl.ds(r0, q), :] + dtbc_ref[...])
        dt_r = _softplus(dtt_ref[c] + dtbr_ref[...])
        la_c = dt_c * a_c
        la_r = dt_r * a_r
        cum_c = sum(jnp.dot(tri, p, preferred_element_type=F32) for p in _split3(la_c))
        cum_r = sum(jnp.dot(p, tri_t, preferred_element_type=F32) for p in _split3(la_r))
        to_end = dt_c * jnp.exp(cum_c[q - 1:q, :] - cum_c)
        dt_x = jnp.dot(dt_c.astype(BF16), widen, preferred_element_type=F32)
        te_x = jnp.dot(to_end.astype(BF16), widen, preferred_element_type=F32)
        ec_x = jnp.dot(jnp.exp(cum_c).astype(BF16), widen, preferred_element_type=F32)

        for g in range(SSD_N_GROUPS):
            bg = conv(r0, SSD_D_INNER + g * SSD_D_STATE, SSD_D_STATE).astype(BF16)
            cg = conv(r0, SSD_D_INNER + SSD_BC_W + g * SSD_D_STATE, SSD_D_STATE).astype(BF16)
            cb = lax.dot_general(cg, bg, (((1,), (1,)), ((), ())),
                                 preferred_element_type=F32)
            ws = []
            for r in range(SSD_HPG):
                h = g * SSD_HPG + r
                cc = jnp.broadcast_to(cum_c[:, h:h + 1], (q, q))
                cr = jnp.broadcast_to(cum_r[h:h + 1, :], (q, q))
                lm = jnp.exp(jnp.where(causal, cc - cr, -jnp.inf))
                ws.append((cb * lm).astype(BF16))
            wcat = jnp.concatenate(ws, axis=1)
            cols = slice(g * SSD_GROUP_W, (g + 1) * SSD_GROUP_W)
            xs_g = conv(r0, g * SSD_GROUP_W, SSD_GROUP_W)
            xdt_b = (xs_g * dt_x[:, cols]).astype(BF16)
            xdtd = (xs_g * te_x[:, cols]).astype(BF16)
            xbd = jnp.concatenate(
                [jnp.where(head_of_lane == r, xdt_b, jnp.zeros_like(xdt_b))
                 for r in range(SSD_HPG)], axis=0)
            y = jnp.dot(wcat, xbd, preferred_element_type=F32)
            prev = st_s[:, cols]
            y = y + jnp.dot(cg, prev.astype(BF16), preferred_element_type=F32) * ec_x[:, cols]
            st_s[:, cols] = prev * ec_x[q - 1:q, cols] + lax.dot_general(
                bg, xdtd, (((0,), (0,)), ((), ())), preferred_element_type=F32)
            y = y + dcol_ref[:, cols] * xs_g
            zg = z_ref[pl.ds(r0, q), cols].astype(F32)
            y = y * _silu(zg)
            y_ref[pl.ds(r0, q), cols] = (_rms(y) * ng_ref[:, cols]).astype(BF16)
        return carry

    lax.fori_loop(0, nch, chunk, 0)


def _ssd(xbc, z, dt, dtt, conv_w, conv_b, dt_bias, a_log, d_ssd, ssd_norm_g, bsz, seqlen):
    tm = SSD_ROW_TILE
    nt = seqlen // tm
    nch = tm // SSD_CHUNK
    h = SSD_N_HEADS
    row = lambda w: pl.BlockSpec((tm, w), lambda b, i: (b * nt + i, 0))
    return pl.pallas_call(
        _ssd_kernel,
        grid=(bsz, nt),
        in_specs=[row(SSD_CONV_DIM), row(SSD_D_INNER), row(h),
                  pl.BlockSpec((nch, h, SSD_CHUNK), lambda b, i: (b * nt + i, 0, 0)),
                  _const_spec((SSD_CONV, SSD_CONV_DIM)), _const_spec((1, SSD_CONV_DIM)),
                  _const_spec((1, h)), _const_spec((h, 1)), _const_spec((1, h)),
                  _const_spec((h, 1)), _const_spec((1, SSD_D_INNER)),
                  _const_spec((1, SSD_D_INNER))],
        out_specs=row(SSD_D_INNER),
        out_shape=jax.ShapeDtypeStruct((bsz * seqlen, SSD_D_INNER), BF16),
        scratch_shapes=[pltpu.VMEM((tm + V7X_SUBLANES, SSD_CONV_DIM), F32),
                        pltpu.VMEM((SSD_D_STATE, SSD_D_INNER), F32)],
        compiler_params=pltpu.CompilerParams(dimension_semantics=("arbitrary", "arbitrary"),
                                             vmem_limit_bytes=V7X_VMEM_LIMIT_BYTES),
        name="ssd",
    )(xbc, z, dt, dtt, conv_w, conv_b.reshape(1, -1), dt_bias.reshape(1, h),
      dt_bias.reshape(h, 1), a_log.reshape(1, h), a_log.reshape(h, 1),
      jnp.repeat(d_ssd, SSD_HEADDIM).reshape(1, SSD_D_INNER), ssd_norm_g.reshape(1, -1))


def _s5_kernel(u_ref, mt_ref, et_ref, ft_ref, pre_ref, pim_ref, d_ref, y_ref,
               ut_s, yt_s, sre, sim, cre, cim):
    n_slab = u_ref.shape[0]
    nblk = u_ref.shape[1] // S5_BLOCK
    npair = S5_N_GROUPS // 2
    gps = V7X_LANES // S5_GROUP
    hw = V7X_LANES

    @pl.when(pl.program_id(1) == 0)
    def _():
        cre[...] = jnp.zeros_like(cre)
        cim[...] = jnp.zeros_like(cim)

    for s in range(S5_BLOCK):
        for j in range(n_slab):
            xt = u_ref[j, pl.ds(s, nblk, stride=S5_BLOCK), :].astype(BF16).T
            for gl in range(gps):
                ut_s[j * gps + gl, s * S5_GROUP:(s + 1) * S5_GROUP, :] = (
                    xt[gl * S5_GROUP:(gl + 1) * S5_GROUP, :])

    for pr in range(npair):
        ucat = jnp.concatenate([ut_s[2 * pr], ut_s[2 * pr + 1]], axis=0)
        sl = jnp.dot(et_ref[pr], ucat, preferred_element_type=F32).T
        sre[pl.ds(pr, nblk, stride=npair), :] = sl[:, :hw]
        sim[pl.ds(pr, nblk, stride=npair), :] = sl[:, hw:]

    a_re = pre_ref[...]
    a_im = pim_ref[...]

    def step(r, carry):
        c_re, c_im = carry
        rows = pl.ds(pl.multiple_of(r * npair, npair), npair)
        l_re = sre[rows, :]
        l_im = sim[rows, :]
        sre[rows, :] = c_re
        sim[rows, :] = c_im
        return (a_re * c_re - a_im * c_im + l_re, a_re * c_im + a_im * c_re + l_im)

    c_re, c_im = lax.fori_loop(0, nblk, step, (cre[...], cim[...]))
    cre[...] = c_re
    cim[...] = c_im

    for pr in range(npair):
        sp = jnp.concatenate([sre[pl.ds(pr, nblk, stride=npair), :],
                              sim[pl.ds(pr, nblk, stride=npair), :]], axis=1).astype(BF16)
        cross = lax.dot_general(ft_ref[pr], sp, (((1,), (1,)), ((), ())),
                                preferred_element_type=F32)
        for j in range(2):
            gi = 2 * pr + j
            ug = ut_s[gi]
            y = (jnp.dot(mt_ref[gi], ug, preferred_element_type=F32)
                 + cross[j * S5_BLOCK_W:(j + 1) * S5_BLOCK_W, :]
                 + d_ref[gi] * ug.astype(F32))
            yt_s[gi] = jax.nn.gelu(y).astype(BF16)

    for t in range(S5_BLOCK):
        for j in range(n_slab):
            zt = jnp.concatenate(
                [yt_s[j * gps + gl, t * S5_GROUP:(t + 1) * S5_GROUP, :] for gl in range(gps)],
                axis=0)
            y_ref[j, pl.ds(t, nblk, stride=S5_BLOCK), :] = zt.T.astype(F32)


def _s5(u_slab, mt, et, ft, pw_re, pw_im, d_col, bsz):
    n_slab, t, _ = u_slab.shape
    nblk = S5_BLOCKS_PER_STEP
    rows = nblk * S5_BLOCK
    parts = t // bsz // rows
    g, w = S5_N_GROUPS, S5_BLOCK_W
    io = pl.BlockSpec((n_slab, rows, V7X_LANES), lambda b, p: (0, b * parts + p, 0))
    state = pltpu.VMEM((nblk * (g // 2), V7X_LANES), F32)
    carry = pltpu.VMEM((g // 2, V7X_LANES), F32)
    return pl.pallas_call(
        _s5_kernel,
        grid=(bsz, parts),
        in_specs=[io, _const_spec(mt.shape), _const_spec(et.shape), _const_spec(ft.shape),
                  _const_spec(pw_re.shape), _const_spec(pw_im.shape), _const_spec(d_col.shape)],
        out_specs=io,
        out_shape=jax.ShapeDtypeStruct(u_slab.shape, F32),
        scratch_shapes=[pltpu.VMEM((g, w, nblk), BF16), pltpu.VMEM((g, w, nblk), BF16),
                        state, state, carry, carry],
        compiler_params=pltpu.CompilerParams(dimension_semantics=("arbitrary", "arbitrary"),
                                             vmem_limit_bytes=V7X_VMEM_LIMIT_BYTES),
        name="s5",
    )(u_slab, mt, et, ft, pw_re, pw_im, d_col)


def _s5_tables(taps, e_re, e_im, f_re, f_im, pw_re, pw_im, s5_d):
    g, q, w = S5_N_GROUPS, S5_BLOCK, S5_BLOCK_W
    m = jnp.stack([jnp.pad(taps[:, :, :w - S5_GROUP * s], ((0, 0), (0, 0), (S5_GROUP * s, 0)))
                   for s in range(q)], axis=1).reshape(g, w, w)
    eg_re = jnp.transpose(e_re, (1, 0, 2, 3)).reshape(g // 2, 2, w, S5_STATE)
    eg_im = jnp.transpose(e_im, (1, 0, 2, 3)).reshape(g // 2, 2, w, S5_STATE)
    zero = jnp.zeros_like(eg_re[:, 0])
    e_top = jnp.concatenate([eg_re[:, 0], zero, eg_im[:, 0], zero], axis=-1)
    e_bot = jnp.concatenate([zero, eg_re[:, 1], zero, eg_im[:, 1]], axis=-1)
    e = jnp.concatenate([e_top, e_bot], axis=1)
    fg_re = jnp.transpose(f_re, (1, 3, 0, 2)).reshape(g // 2, 2, S5_STATE, w)
    fg_im = jnp.transpose(f_im, (1, 3, 0, 2)).reshape(g // 2, 2, S5_STATE, w)
    zf = jnp.zeros_like(fg_re[:, 0])
    f = jnp.concatenate([
        jnp.concatenate([fg_re[:, 0], zf], axis=-1),
        jnp.concatenate([zf, fg_re[:, 1]], axis=-1),
        jnp.concatenate([fg_im[:, 0], zf], axis=-1),
        jnp.concatenate([zf, fg_im[:, 1]], axis=-1)], axis=1)
    tr = lambda a: jnp.swapaxes(a, 1, 2).astype(BF16)
    pw_re = pw_re.reshape(g // 2, 2 * S5_STATE)
    pw_im = pw_im.reshape(g // 2, 2 * S5_STATE)
    d_col = jnp.tile(s5_d.reshape(g, 1, S5_GROUP), (1, q, 1)).reshape(g, w, 1)
    return tr(m), tr(e), tr(f), pw_re, pw_im, d_col


def _merge_mlp_kernel(x_ref, ya_ref, yb_ref, gate_ref, gw_ref, gb_ref, wa_ref, wb_ref, wo_ref,
                      g2_ref, wi_ref, wo2_ref, g3_ref, o_ref):
    ybf = jnp.concatenate([yb_ref[j] for j in range(yb_ref.shape[0])], axis=1)
    yb = ybf.astype(BF16)
    glu = jnp.dot(yb, gw_ref[...], preferred_element_type=F32) + gb_ref[...]
    ybg = (ybf * jax.nn.sigmoid(glu)).astype(BF16)
    pa = jnp.dot(ya_ref[...], wa_ref[...], preferred_element_type=F32)
    pb = jnp.dot(ybg, wb_ref[...], preferred_element_type=F32)
    merged = (gate_ref[:, :D_MODEL].astype(F32) * pa
              + gate_ref[:, D_MODEL:].astype(F32) * pb).astype(BF16)
    x1 = x_ref[...] + jnp.dot(merged, wo_ref[...], preferred_element_type=F32)
    h2 = (_rms(x1) * g2_ref[...]).astype(BF16)
    acc = x1
    for k in range(D_FF // D_MODEL):
        cols = slice(k * D_MODEL, (k + 1) * D_MODEL)
        hid = jnp.dot(h2, wi_ref[:, cols], preferred_element_type=F32)
        hid = jnp.square(jnp.maximum(hid, 0.0)).astype(BF16)
        acc = acc + jnp.dot(hid, wo2_ref[cols, :], preferred_element_type=F32)
    o_ref[...] = _rms(acc) * g3_ref[...]


def _merge_mlp(x2, ya, yb_slab, gates, glu_w, glu_b, w_branch, w_out, g_mlp, w_mlp_in,
               w_mlp_out, g_final):
    t = x2.shape[0]
    tm = ROW_TILE
    wa = w_branch[:SSD_D_INNER].astype(BF16)
    wb = w_branch[SSD_D_INNER:].astype(BF16)
    row = lambda w: pl.BlockSpec((tm, w), lambda i: (i, 0))
    vec = lambda v: v.reshape(1, -1)
    return pl.pallas_call(
        _merge_mlp_kernel,
        grid=(t // tm,),
        in_specs=[row(D_MODEL), row(SSD_D_INNER),
                  pl.BlockSpec((yb_slab.shape[0], tm, V7X_LANES), lambda i: (0, i, 0)),
                  row(N_BRANCHES * D_MODEL),
                  _const_spec((S5_WIDTH, S5_WIDTH)), _const_spec((1, S5_WIDTH)),
                  _const_spec(wa.shape), _const_spec(wb.shape), _const_spec((D_MODEL, D_MODEL)),
                  _const_spec((1, D_MODEL)), _const_spec((D_MODEL, D_FF)),
                  _const_spec((D_FF, D_MODEL)), _const_spec((1, D_MODEL))],
        out_specs=row(D_MODEL),
        out_shape=jax.ShapeDtypeStruct((t, D_MODEL), F32),
        compiler_params=pltpu.CompilerParams(dimension_semantics=("arbitrary",),
                                             vmem_limit_bytes=V7X_VMEM_LIMIT_BYTES),
        name="merge_mlp",
    )(x2, ya, yb_slab, gates, glu_w.astype(BF16), vec(glu_b), wa, wb, w_out.astype(BF16),
      vec(g_mlp), w_mlp_in.astype(BF16), w_mlp_out.astype(BF16), vec(g_final))


def kernel(x, norm_mix_g, w_in, conv_w, conv_b, dt_bias, a_log, d_ssd, ssd_norm_g, s5_a_re, s5_a_im, s5_log_dt, s5_b_re, s5_b_im, s5_c_re, s5_c_im, s5_d, s5_glu_w, s5_glu_b, w_branch, w_out, norm_mlp_g, w_mlp_in, w_mlp_out, norm_final_g):
    bsz, seqlen, _ = x.shape
    t = bsz * seqlen
    assert seqlen % SSD_ROW_TILE == 0 and seqlen % ROW_TILE == 0
    assert seqlen % (S5_BLOCKS_PER_STEP * S5_BLOCK) == 0
    x2 = x.reshape(t, D_MODEL)

    z, xbc, dt, dtt, u_slab, gates = _in_proj(x2, norm_mix_g, w_in)
    ya = _ssd(xbc, z, dt, dtt, conv_w, conv_b, dt_bias, a_log, d_ssd, ssd_norm_g, bsz, seqlen)
    tables = _s5_tables(*_s5_prep(s5_a_re, s5_a_im, s5_log_dt, s5_b_re, s5_b_im,
                                  s5_c_re, s5_c_im), s5_d)
    yb_slab = _s5(u_slab, *tables, bsz)
    out = _merge_mlp(x2, ya, yb_slab, gates, s5_glu_w, s5_glu_b, w_branch, w_out, norm_mlp_g,
                     w_mlp_in, w_mlp_out, norm_final_g)
    return out.reshape(bsz, seqlen, D_MODEL)
```

```python
import jax
import jax.numpy as jnp
from jax import lax
from jax.experimental import pallas as pl
from jax.experimental.pallas import tpu as pltpu

F32 = jnp.float32
BF16 = jnp.bfloat16

D_MODEL = 1024
SSD_D_INNER = D_MODEL
SSD_HEADDIM = 64
SSD_N_HEADS = SSD_D_INNER // SSD_HEADDIM
SSD_N_GROUPS = 4
SSD_HPG = SSD_N_HEADS // SSD_N_GROUPS
SSD_D_STATE = 128
SSD_CONV = 4
SSD_CHUNK = 128
SSD_GROUP_W = SSD_D_INNER // SSD_N_GROUPS
SSD_BC_W = SSD_N_GROUPS * SSD_D_STATE
SSD_CONV_DIM = SSD_D_INNER + 2 * SSD_BC_W
S5_WIDTH = D_MODEL // 2
S5_GROUP = 16
S5_N_GROUPS = S5_WIDTH // S5_GROUP
S5_STATE = 64
S5_BLOCK = 16
S5_BLOCK_W = S5_BLOCK * S5_GROUP
D_FF = 4 * D_MODEL
N_BRANCHES = 2
OFF_Z = 0
OFF_XBC = OFF_Z + SSD_D_INNER
OFF_DT = OFF_XBC + SSD_CONV_DIM
OFF_U = OFF_DT + SSD_N_HEADS
OFF_G = OFF_U + S5_WIDTH
EPS = 1e-6

V7X_LANES = 128
V7X_SUBLANES = 8
V7X_VMEM_LIMIT_BYTES = 56 * 1024 * 1024

ROW_TILE = 512
SSD_ROW_TILE = 512
S5_BLOCKS_PER_STEP = 128
CONV_ROWS = 256


def _const_spec(shape):
    zeros = (0,) * len(shape)
    return pl.BlockSpec(shape, lambda *_: zeros, pipeline_mode=pl.Buffered(1))


def _rms(x):
    return x * lax.rsqrt(jnp.mean(x * x, axis=-1, keepdims=True) + EPS)


def _split3(x):
    hi = x.astype(BF16)
    r1 = x - hi.astype(F32)
    mid = r1.astype(BF16)
    lo = (r1 - mid.astype(F32)).astype(BF16)
    return hi, mid, lo


def _softplus(x):
    return jnp.maximum(x, 0.0) + jnp.log(1.0 + jnp.exp(-jnp.abs(x)))


def _silu(x):
    h = 0.5 * x
    return h + h * jnp.tanh(h)


def _s5_prep_kernel(are_ref, aim_ref, ldt_ref, bre_ref, bim_ref, cre_ref, cim_ref,
                    taps_ref, ere_ref, eim_ref, fre_ref, fim_ref, pwre_ref, pwim_ref):
    a_re = are_ref[...]
    a_im = aim_ref[...]
    dt = jnp.exp(ldt_ref[...])
    mag = jnp.exp(a_re * dt)
    ab_re = mag * jnp.cos(a_im * dt)
    ab_im = mag * jnp.sin(a_im * dt)
    den = a_re * a_re + a_im * a_im
    nr = ab_re - 1.0
    ni = ab_im
    coef_re = (nr * a_re + ni * a_im) / den
    coef_im = (ni * a_re - nr * a_im) / den
    b_re = bre_ref[...]
    b_im = bim_ref[...]
    bb_re = coef_re * b_re - coef_im * b_im
    bb_im = coef_re * b_im + coef_im * b_re
    c_re = cre_ref[...]
    c_im = cim_ref[...]

    p_re = jnp.ones_like(ab_re)
    p_im = jnp.zeros_like(ab_re)
    ca_re = []
    ca_im = []
    for tau in range(S5_BLOCK + 1):
        ca_re.append(c_re * p_re - c_im * p_im)
        ca_im.append(c_re * p_im + c_im * p_re)
        if tau < S5_BLOCK:
            s = S5_BLOCK - 1 - tau
            ere_ref[s] = p_re * bb_re - p_im * bb_im
            eim_ref[s] = p_re * bb_im + p_im * bb_re
        if tau >= 1:
            fre_ref[tau - 1] = ca_re[tau]
            fim_ref[tau - 1] = -ca_im[tau]
        if tau == S5_BLOCK:
            pwre_ref[...] = p_re
            pwim_ref[...] = p_im
        p_re, p_im = (p_re * ab_re - p_im * ab_im, p_re * ab_im + p_im * ab_re)

    car = jnp.concatenate(ca_re[:S5_BLOCK], axis=1)
    cai = jnp.concatenate(ca_im[:S5_BLOCK], axis=1)
    dn = (((2,), (2,)), ((0,), (0,)))
    taps = (lax.dot_general(bb_re, car, dn, precision=lax.Precision.HIGHEST,
                            preferred_element_type=F32)
            - lax.dot_general(bb_im, cai, dn, precision=lax.Precision.HIGHEST,
                              preferred_element_type=F32))
    taps_ref[...] = taps


def _s5_prep(a_re, a_im, log_dt, b_re, b_im, c_re, c_im):
    g = S5_N_GROUPS
    bt_re = jnp.transpose(b_re, (0, 2, 1))
    bt_im = jnp.transpose(b_im, (0, 2, 1))
    tab = jax.ShapeDtypeStruct((S5_BLOCK, g, S5_GROUP, S5_STATE), F32)
    return pl.pallas_call(
        _s5_prep_kernel,
        out_shape=(jax.ShapeDtypeStruct((g, S5_GROUP, S5_BLOCK_W), F32), tab, tab, tab, tab,
                   jax.ShapeDtypeStruct((g, 1, S5_STATE), F32),
                   jax.ShapeDtypeStruct((g, 1, S5_STATE), F32)),
        name="s5_prep",
    )(a_re.reshape(g, 1, S5_STATE), a_im.reshape(g, 1, S5_STATE), log_dt.reshape(g, 1, 1),
      bt_re, bt_im, c_re, c_im)


def _in_proj_kernel(x_ref, g_ref, wz_ref, wxbct_ref, wdt_ref, wu_ref, wg_ref,
                    zs_ref, xbct_ref, dt_ref, dtt_ref, u_ref, gate_ref):
    h = (_rms(x_ref[...]) * g_ref[...]).astype(BF16)
    xbct_ref[...] = lax.dot_general(wxbct_ref[...], h, (((1,), (1,)), ((), ())),
                                    preferred_element_type=F32).astype(BF16)
    zs_ref[...] = _silu(jnp.dot(h, wz_ref[...], preferred_element_type=F32)).astype(BF16)
    u = jnp.dot(h, wu_ref[...], preferred_element_type=F32)
    for j in range(u_ref.shape[0]):
        u_ref[j] = u[:, j * V7X_LANES:(j + 1) * V7X_LANES]
    gate_ref[...] = jax.nn.sigmoid(
        jnp.dot(h, wg_ref[...], preferred_element_type=F32)).astype(BF16)
    dt = jnp.dot(h, wdt_ref[...], preferred_element_type=F32)
    dt_ref[...] = dt[:, :SSD_N_HEADS]
    for c in range(dtt_ref.shape[0]):
        dtt_ref[c] = dt[c * SSD_CHUNK:(c + 1) * SSD_CHUNK, :].T[:SSD_N_HEADS, :]


def _in_proj(x2, g_mix, w_in):
    t = x2.shape[0]
    tm = ROW_TILE
    wz = w_in[:, OFF_Z:OFF_XBC].astype(BF16)
    wxbct = w_in[:, OFF_XBC:OFF_DT].T.astype(BF16)
    wdt_pad = jnp.pad(w_in[:, OFF_DT:OFF_U],
                      ((0, 0), (0, V7X_LANES - SSD_N_HEADS))).astype(BF16)
    wu = w_in[:, OFF_U:OFF_G].astype(BF16)
    wg = w_in[:, OFF_G:].astype(BF16)
    n_slab = S5_WIDTH // V7X_LANES
    row = lambda w: pl.BlockSpec((tm, w), lambda i: (i, 0))
    return pl.pallas_call(
        _in_proj_kernel,
        grid=(t // tm,),
        in_specs=[row(D_MODEL), _const_spec((1, D_MODEL)), _const_spec(wz.shape),
                  _const_spec(wxbct.shape), _const_spec(wdt_pad.shape),
                  _const_spec(wu.shape), _const_spec(wg.shape)],
        out_specs=[row(SSD_D_INNER),
                   pl.BlockSpec((SSD_CONV_DIM, tm), lambda i: (0, i)),
                   row(SSD_N_HEADS),
                   pl.BlockSpec((tm // SSD_CHUNK, SSD_N_HEADS, SSD_CHUNK), lambda i: (i, 0, 0)),
                   pl.BlockSpec((n_slab, tm, V7X_LANES), lambda i: (0, i, 0)),
                   row(N_BRANCHES * D_MODEL)],
        out_shape=(jax.ShapeDtypeStruct((t, SSD_D_INNER), BF16),
                   jax.ShapeDtypeStruct((SSD_CONV_DIM, t), BF16),
                   jax.ShapeDtypeStruct((t, SSD_N_HEADS), F32),
                   jax.ShapeDtypeStruct((t // SSD_CHUNK, SSD_N_HEADS, SSD_CHUNK), F32),
                   jax.ShapeDtypeStruct((n_slab, t, V7X_LANES), F32),
                   jax.ShapeDtypeStruct((t, N_BRANCHES * D_MODEL), BF16)),
        compiler_params=pltpu.CompilerParams(dimension_semantics=("arbitrary",),
                                             vmem_limit_bytes=V7X_VMEM_LIMIT_BYTES),
        name="in_proj",
    )(x2, g_mix.reshape(1, D_MODEL), wz, wxbct, wdt_pad, wu, wg)


def _ssd_kernel(xbct_ref, zs_ref, dt_ref, dtt_ref, cwl_ref, cbl_ref, dtbc_ref, dtbr_ref,
                ac_ref, ar_ref, dcol_ref, ng_ref, y_ref, xraw, ht_s, st_s):
    tm = xbct_ref.shape[1]
    nch = tm // SSD_CHUNK
    q = SSD_CHUNK
    first = pl.program_id(1) == 0

    @pl.when(first)
    def _():
        st_s[...] = jnp.zeros_like(st_s)
        xraw[:, 0:q] = jnp.zeros((SSD_CONV_DIM, q), BF16)

    @pl.when(jnp.logical_not(first))
    def _():
        xraw[:, 0:q] = xraw[:, tm:tm + q]

    xraw[:, q:q + tm] = xbct_ref[...]

    sh_r = lax.broadcasted_iota(jnp.int32, (2 * q, q), 0)
    sh_c = lax.broadcasted_iota(jnp.int32, (2 * q, q), 1)
    shift_all = jnp.concatenate(
        [(sh_r == sh_c + (q - (SSD_CONV - 1) + k)).astype(BF16) for k in range(SSD_CONV - 1)],
        axis=1)

    def conv_t(l0, c0, n):
        win = xraw[c0:c0 + n, pl.ds(l0, 2 * q)]
        sh = jnp.dot(win, shift_all, preferred_element_type=F32)
        acc = cbl_ref[c0:c0 + n, :] + cwl_ref[SSD_CONV - 1, c0:c0 + n, :] * win[:, q:].astype(F32)
        for k in range(SSD_CONV - 1):
            acc = acc + cwl_ref[k, c0:c0 + n, :] * sh[:, k * q:(k + 1) * q]
        return acc + acc * jnp.tanh(acc)

    row_i = lax.broadcasted_iota(jnp.int32, (q, q), 0)
    col_i = lax.broadcasted_iota(jnp.int32, (q, q), 1)
    causal = row_i >= col_i
    tri = causal.astype(BF16)
    tri_t = (row_i <= col_i).astype(BF16)
    head_of_lane = lax.broadcasted_iota(jnp.int32, (q, SSD_GROUP_W), 1) // SSD_HEADDIM
    widen = (lax.broadcasted_iota(jnp.int32, (SSD_N_HEADS, SSD_D_INNER), 1) // SSD_HEADDIM
             == lax.broadcasted_iota(jnp.int32, (SSD_N_HEADS, SSD_D_INNER), 0)).astype(BF16)
    a_c = -jnp.exp(ac_ref[...])
    a_r = -jnp.exp(ar_ref[...])

    def chunk(c, carry):
        r0 = pl.multiple_of(c * q, q)
        for c0 in range(0, SSD_CONV_DIM, CONV_ROWS):
            ht_s[c0:c0 + CONV_ROWS, :] = conv_t(r0, c0, CONV_ROWS)
        dt_c = _softplus(dt_ref[pl.ds(r0, q), :] + dtbc_ref[...])
        dt_r = _softplus(dtt_ref[c] + dtbr_ref[...])
        la_c = dt_c * a_c
        la_r = dt_r * a_r
        cum_c = sum(jnp.dot(tri, p, preferred_element_type=F32) for p in _split3(la_c))
        cum_r = sum(jnp.dot(p, tri_t, preferred_element_type=F32) for p in _split3(la_r))
        to_end = dt_c * jnp.exp(cum_c[q - 1:q, :] - cum_c)
        dt_x = jnp.dot(dt_c.astype(BF16), widen, preferred_element_type=F32)
        te_x = jnp.dot(to_end.astype(BF16), widen, preferred_element_type=F32)
        ec_x = jnp.dot(jnp.exp(cum_c).astype(BF16), widen, preferred_element_type=F32)

        for g in range(SSD_N_GROUPS):
            b0 = SSD_D_INNER + g * SSD_D_STATE
            c0 = SSD_D_INNER + SSD_BC_W + g * SSD_D_STATE
            bgt = ht_s[b0:b0 + SSD_D_STATE, :].astype(BF16)
            cg = ht_s[c0:c0 + SSD_D_STATE, :].T.astype(BF16)
            cb = jnp.dot(cg, bgt, preferred_element_type=F32)
            ws = []
            for r in range(SSD_HPG):
                h = g * SSD_HPG + r
                cc = jnp.broadcast_to(cum_c[:, h:h + 1], (q, q))
                cr = jnp.broadcast_to(cum_r[h:h + 1, :], (q, q))
                lm = jnp.exp(jnp.where(causal, cc - cr, -jnp.inf))
                ws.append((cb * lm).astype(BF16))
            wcat = jnp.concatenate(ws, axis=1)
            cols = slice(g * SSD_GROUP_W, (g + 1) * SSD_GROUP_W)
            xs_g = ht_s[g * SSD_GROUP_W:(g + 1) * SSD_GROUP_W, :].T
            xdt_b = (xs_g * dt_x[:, cols]).astype(BF16)
            xdtd = (xs_g * te_x[:, cols]).astype(BF16)
            xbd = jnp.concatenate(
                [jnp.where(head_of_lane == r, xdt_b, jnp.zeros_like(xdt_b))
                 for r in range(SSD_HPG)], axis=0)
            y = jnp.dot(wcat, xbd, preferred_element_type=F32)
            prev = st_s[:, cols]
            y = y + jnp.dot(cg, prev.astype(BF16), preferred_element_type=F32) * ec_x[:, cols]
            st_s[:, cols] = prev * ec_x[q - 1:q, cols] + jnp.dot(
                bgt, xdtd, preferred_element_type=F32)
            y = y + dcol_ref[:, cols] * xs_g
            y = y * zs_ref[pl.ds(r0, q), cols].astype(F32)
            y_ref[pl.ds(r0, q), cols] = (_rms(y) * ng_ref[:, cols]).astype(BF16)
        return carry

    lax.fori_loop(0, nch, chunk, 0, unroll=True)


def _ssd(xbct, zs, dt, dtt, conv_w, conv_b, dt_bias, a_log, d_ssd, ssd_norm_g, bsz, seqlen):
    tm = SSD_ROW_TILE
    nt = seqlen // tm
    nch = tm // SSD_CHUNK
    h = SSD_N_HEADS
    row = lambda w: pl.BlockSpec((tm, w), lambda b, i: (b * nt + i, 0))
    cwl = jnp.broadcast_to((0.5 * conv_w)[:, :, None], (SSD_CONV, SSD_CONV_DIM, V7X_LANES))
    cbl = jnp.broadcast_to((0.5 * conv_b)[:, None], (SSD_CONV_DIM, V7X_LANES))
    return pl.pallas_call(
        _ssd_kernel,
        grid=(bsz, nt),
        in_specs=[pl.BlockSpec((SSD_CONV_DIM, tm), lambda b, i: (0, b * nt + i)),
                  row(SSD_D_INNER), row(h),
                  pl.BlockSpec((nch, h, SSD_CHUNK), lambda b, i: (b * nt + i, 0, 0)),
                  _const_spec(cwl.shape), _const_spec(cbl.shape),
                  _const_spec((1, h)), _const_spec((h, 1)), _const_spec((1, h)),
                  _const_spec((h, 1)), _const_spec((1, SSD_D_INNER)),
                  _const_spec((1, SSD_D_INNER))],
        out_specs=row(SSD_D_INNER),
        out_shape=jax.ShapeDtypeStruct((bsz * seqlen, SSD_D_INNER), BF16),
        scratch_shapes=[pltpu.VMEM((SSD_CONV_DIM, SSD_CHUNK + tm), BF16),
                        pltpu.VMEM((SSD_CONV_DIM, SSD_CHUNK), F32),
                        pltpu.VMEM((SSD_D_STATE, SSD_D_INNER), F32)],
        compiler_params=pltpu.CompilerParams(dimension_semantics=("arbitrary", "arbitrary"),
                                             vmem_limit_bytes=V7X_VMEM_LIMIT_BYTES),
        name="ssd",
    )(xbct, zs, dt, dtt, cwl, cbl, dt_bias.reshape(1, h), dt_bias.reshape(h, 1),
      a_log.reshape(1, h), a_log.reshape(h, 1),
      jnp.repeat(d_ssd, SSD_HEADDIM).reshape(1, SSD_D_INNER), ssd_norm_g.reshape(1, -1))


def _s5_kernel(u_ref, mt_ref, et_ref, ft_ref, pre_ref, pim_ref, d_ref, y_ref,
               ut_s, yt_s, sre, sim, cre, cim):
    n_slab = u_ref.shape[0]
    nblk = u_ref.shape[1] // S5_BLOCK
    npair = S5_N_GROUPS // 2
    gps = V7X_LANES // S5_GROUP
    hw = V7X_LANES

    @pl.when(pl.program_id(1) == 0)
    def _():
        cre[...] = jnp.zeros_like(cre)
        cim[...] = jnp.zeros_like(cim)

    for s in range(S5_BLOCK):
        for j in range(n_slab):
            xt = u_ref[j, pl.ds(s, nblk, stride=S5_BLOCK), :].astype(BF16).T
            for gl in range(gps):
                ut_s[j * gps + gl, s * S5_GROUP:(s + 1) * S5_GROUP, :] = (
                    xt[gl * S5_GROUP:(gl + 1) * S5_GROUP, :])

    for pr in range(npair):
        ucat = jnp.concatenate([ut_s[2 * pr], ut_s[2 * pr + 1]], axis=0)
        sl = jnp.dot(et_ref[pr], ucat, preferred_element_type=F32).T
        sre[pl.ds(pr, nblk, stride=npair), :] = sl[:, :hw]
        sim[pl.ds(pr, nblk, stride=npair), :] = sl[:, hw:]

    a_re = pre_ref[...]
    a_im = pim_ref[...]

    def step(r, carry):
        c_re, c_im = carry
        rows = pl.ds(pl.multiple_of(r * npair, npair), npair)
        l_re = sre[rows, :]
        l_im = sim[rows, :]
        sre[rows, :] = c_re
        sim[rows, :] = c_im
        return (a_re * c_re - a_im * c_im + l_re, a_re * c_im + a_im * c_re + l_im)

    c_re, c_im = lax.fori_loop(0, nblk, step, (cre[...], cim[...]))
    cre[...] = c_re
    cim[...] = c_im

    for pr in range(npair):
        sp = jnp.concatenate([sre[pl.ds(pr, nblk, stride=npair), :],
                              sim[pl.ds(pr, nblk, stride=npair), :]], axis=1).astype(BF16)
        cross = lax.dot_general(ft_ref[pr], sp, (((1,), (1,)), ((), ())),
                                preferred_element_type=F32)
        for j in range(2):
            gi = 2 * pr + j
            ug = ut_s[gi]
            y = (jnp.dot(mt_ref[gi], ug, preferred_element_type=F32)
                 + cross[j * S5_BLOCK_W:(j + 1) * S5_BLOCK_W, :]
                 + d_ref[gi] * ug.astype(F32))
            yt_s[gi] = jax.nn.gelu(y).astype(BF16)

    for t in range(S5_BLOCK):
        for j in range(n_slab):
            zt = jnp.concatenate(
                [yt_s[j * gps + gl, t * S5_GROUP:(t + 1) * S5_GROUP, :] for gl in range(gps)],
                axis=0)
            y_ref[j, pl.ds(t, nblk, stride=S5_BLOCK), :] = zt.T.astype(F32)


def _s5(u_slab, mt, et, ft, pw_re, pw_im, d_col, bsz):
    n_slab, t, _ = u_slab.shape
    nblk = S5_BLOCKS_PER_STEP
    rows = nblk * S5_BLOCK
    parts = t // bsz // rows
    g, w = S5_N_GROUPS, S5_BLOCK_W
    io = pl.BlockSpec((n_slab, rows, V7X_LANES), lambda b, p: (0, b * parts + p, 0))
    state = pltpu.VMEM((nblk * (g // 2), V7X_LANES), F32)
    carry = pltpu.VMEM((g // 2, V7X_LANES), F32)
    return pl.pallas_call(
        _s5_kernel,
        grid=(bsz, parts),
        in_specs=[io, _const_spec(mt.shape), _const_spec(et.shape), _const_spec(ft.shape),
                  _const_spec(pw_re.shape), _const_spec(pw_im.shape), _const_spec(d_col.shape)],
        out_specs=io,
        out_shape=jax.ShapeDtypeStruct(u_slab.shape, F32),
        scratch_shapes=[pltpu.VMEM((g, w, nblk), BF16), pltpu.VMEM((g, w, nblk), BF16),
                        state, state, carry, carry],
        compiler_params=pltpu.CompilerParams(dimension_semantics=("arbitrary", "arbitrary"),
                                             vmem_limit_bytes=V7X_VMEM_LIMIT_BYTES),
        name="s5",
    )(u_slab, mt, et, ft, pw_re, pw_im, d_col)


def _s5_tables(taps, e_re, e_im, f_re, f_im, pw_re, pw_im, s5_d):
    g, q, w = S5_N_GROUPS, S5_BLOCK, S5_BLOCK_W
    m = jnp.stack([jnp.pad(taps[:, :, :w - S5_GROUP * s], ((0, 0), (0, 0), (S5_GROUP * s, 0)))
                   for s in range(q)], axis=1).reshape(g, w, w)
    eg_re = jnp.transpose(e_re, (1, 0, 2, 3)).reshape(g // 2, 2, w, S5_STATE)
    eg_im = jnp.transpose(e_im, (1, 0, 2, 3)).reshape(g // 2, 2, w, S5_STATE)
    zero = jnp.zeros_like(eg_re[:, 0])
    e_top = jnp.concatenate([eg_re[:, 0], zero, eg_im[:, 0], zero], axis=-1)
    e_bot = jnp.concatenate([zero, eg_re[:, 1], zero, eg_im[:, 1]], axis=-1)
    e = jnp.concatenate([e_top, e_bot], axis=1)
    fg_re = jnp.transpose(f_re, (1, 3, 0, 2)).reshape(g // 2, 2, S5_STATE, w)
    fg_im = jnp.transpose(f_im, (1, 3, 0, 2)).reshape(g // 2, 2, S5_STATE, w)
    zf = jnp.zeros_like(fg_re[:, 0])
    f = jnp.concatenate([
        jnp.concatenate([fg_re[:, 0], zf], axis=-1),
        jnp.concatenate([zf, fg_re[:, 1]], axis=-1),
        jnp.concatenate([fg_im[:, 0], zf], axis=-1),
        jnp.concatenate([zf, fg_im[:, 1]], axis=-1)], axis=1)
    tr = lambda a: jnp.swapaxes(a, 1, 2).astype(BF16)
    pw_re = pw_re.reshape(g // 2, 2 * S5_STATE)
    pw_im = pw_im.reshape(g // 2, 2 * S5_STATE)
    d_col = jnp.tile(s5_d.reshape(g, 1, S5_GROUP), (1, q, 1)).reshape(g, w, 1)
    return tr(m), tr(e), tr(f), pw_re, pw_im, d_col


def _merge_mlp_kernel(x_ref, ya_ref, yb_ref, gate_ref, gw_ref, gb_ref, wa_ref, wb_ref, wo_ref,
                      g2_ref, wi_ref, wo2_ref, g3_ref, o_ref):
    ybf = jnp.concatenate([yb_ref[j] for j in range(yb_ref.shape[0])], axis=1)
    yb = ybf.astype(BF16)
    glu = jnp.dot(yb, gw_ref[...], preferred_element_type=F32) + gb_ref[...]
    ybg = (ybf * jax.nn.sigmoid(glu)).astype(BF16)
    pa = jnp.dot(ya_ref[...], wa_ref[...], preferred_element_type=F32)
    pb = jnp.dot(ybg, wb_ref[...], preferred_element_type=F32)
    merged = (gate_ref[:, :D_MODEL].astype(F32) * pa
              + gate_ref[:, D_MODEL:].astype(F32) * pb).astype(BF16)
    x1 = x_ref[...] + jnp.dot(merged, wo_ref[...], preferred_element_type=F32)
    h2 = (_rms(x1) * g2_ref[...]).astype(BF16)
    acc = x1
    for k in range(D_FF // D_MODEL):
        cols = slice(k * D_MODEL, (k + 1) * D_MODEL)
        hid = jnp.dot(h2, wi_ref[:, cols], preferred_element_type=F32)
        hid = jnp.square(jnp.maximum(hid, 0.0)).astype(BF16)
        acc = acc + jnp.dot(hid, wo2_ref[cols, :], preferred_element_type=F32)
    o_ref[...] = _rms(acc) * g3_ref[...]


def _merge_mlp(x2, ya, yb_slab, gates, glu_w, glu_b, w_branch, w_out, g_mlp, w_mlp_in,
               w_mlp_out, g_final):
    t = x2.shape[0]
    tm = ROW_TILE
    wa = w_branch[:SSD_D_INNER].astype(BF16)
    wb = w_branch[SSD_D_INNER:].astype(BF16)
    row = lambda w: pl.BlockSpec((tm, w), lambda i: (i, 0))
    vec = lambda v: v.reshape(1, -1)
    return pl.pallas_call(
        _merge_mlp_kernel,
        grid=(t // tm,),
        in_specs=[row(D_MODEL), row(SSD_D_INNER),
                  pl.BlockSpec((yb_slab.shape[0], tm, V7X_LANES), lambda i: (0, i, 0)),
                  row(N_BRANCHES * D_MODEL),
                  _const_spec((S5_WIDTH, S5_WIDTH)), _const_spec((1, S5_WIDTH)),
                  _const_spec(wa.shape), _const_spec(wb.shape), _const_spec((D_MODEL, D_MODEL)),
                  _const_spec((1, D_MODEL)), _const_spec((D_MODEL, D_FF)),
                  _const_spec((D_FF, D_MODEL)), _const_spec((1, D_MODEL))],
        out_specs=row(D_MODEL),
        out_shape=jax.ShapeDtypeStruct((t, D_MODEL), F32),
        compiler_params=pltpu.CompilerParams(dimension_semantics=("arbitrary",),
                                             vmem_limit_bytes=V7X_VMEM_LIMIT_BYTES),
        name="merge_mlp",
    )(x2, ya, yb_slab, gates, glu_w.astype(BF16), vec(glu_b), wa, wb, w_out.astype(BF16),
      vec(g_mlp), w_mlp_in.astype(BF16), w_mlp_out.astype(BF16), vec(g_final))


def kernel(x, norm_mix_g, w_in, conv_w, conv_b, dt_bias, a_log, d_ssd, ssd_norm_g, s5_a_re, s5_a_im, s5_log_dt, s5_b_re, s5_b_im, s5_c_re, s5_c_im, s5_d, s5_glu_w, s5_glu_b, w_branch, w_out, norm_mlp_g, w_mlp_in, w_mlp_out, norm_final_g):
    bsz, seqlen, _ = x.shape
    t = bsz * seqlen
    assert seqlen % SSD_ROW_TILE == 0 and seqlen % ROW_TILE == 0
    assert seqlen % (S5_BLOCKS_PER_STEP * S5_BLOCK) == 0
    x2 = x.reshape(t, D_MODEL)

    zs, xbct, dt, dtt, u_slab, gates = _in_proj(x2, norm_mix_g, w_in)
    ya = _ssd(xbct, zs, dt, dtt, conv_w, conv_b, dt_bias, a_log, d_ssd, ssd_norm_g, bsz,
              seqlen)
    tables = _s5_tables(*_s5_prep(s5_a_re, s5_a_im, s5_log_dt, s5_b_re, s5_b_im,
                                  s5_c_re, s5_c_im), s5_d)
    yb_slab = _s5(u_slab, *tables, bsz)
    out = _merge_mlp(x2, ya, yb_slab, gates, s5_glu_w, s5_glu_b, w_branch, w_out, norm_mlp_g,
                     w_mlp_in, w_mlp_out, norm_final_g)
    return out.reshape(bsz, seqlen, D_MODEL)
```

```python
import jax
import jax.numpy as jnp
from jax import lax
from jax.experimental import pallas as pl
from jax.experimental.pallas import tpu as pltpu

F32 = jnp.float32
BF16 = jnp.bfloat16

D_MODEL = 1024
SSD_D_INNER = D_MODEL
SSD_HEADDIM = 64
SSD_N_HEADS = SSD_D_INNER // SSD_HEADDIM
SSD_N_GROUPS = 4
SSD_HPG = SSD_N_HEADS // SSD_N_GROUPS
SSD_D_STATE = 128
SSD_CONV = 4
SSD_CHUNK = 128
SSD_GROUP_W = SSD_D_INNER // SSD_N_GROUPS
SSD_BC_W = SSD_N_GROUPS * SSD_D_STATE
SSD_CONV_DIM = SSD_D_INNER + 2 * SSD_BC_W
S5_WIDTH = D_MODEL // 2
S5_GROUP = 16
S5_N_GROUPS = S5_WIDTH // S5_GROUP
S5_STATE = 64
S5_BLOCK = 16
S5_BLOCK_W = S5_BLOCK * S5_GROUP
D_FF = 4 * D_MODEL
N_BRANCHES = 2
OFF_Z = 0
OFF_XBC = OFF_Z + SSD_D_INNER
OFF_DT = OFF_XBC + SSD_CONV_DIM
OFF_U = OFF_DT + SSD_N_HEADS
OFF_G = OFF_U + S5_WIDTH
EPS = 1e-6

V7X_LANES = 128
V7X_SUBLANES = 8
V7X_VMEM_LIMIT_BYTES = 56 * 1024 * 1024

ROW_TILE = 512
IN_PROJ_ROW_TILE = 1024
SSD_ROW_TILE = 512
S5_BLOCKS_PER_STEP = 128
CONV_ROWS = 256
S5_STATE_PITCH = 24


def _const_spec(shape):
    zeros = (0,) * len(shape)
    return pl.BlockSpec(shape, lambda *_: zeros, pipeline_mode=pl.Buffered(1))


def _rms(x):
    return x * lax.rsqrt(jnp.mean(x * x, axis=-1, keepdims=True) + EPS)


def _split3(x):
    hi = x.astype(BF16)
    r1 = x - hi.astype(F32)
    mid = r1.astype(BF16)
    lo = (r1 - mid.astype(F32)).astype(BF16)
    return hi, mid, lo


def _softplus(x):
    return jnp.maximum(x, 0.0) + jnp.log(1.0 + jnp.exp(-jnp.abs(x)))


GELU_C0 = 0.7978845608028654
GELU_C1 = GELU_C0 * 0.044715


def _gelu_tanh(x):
    h = 0.5 * x
    return h + h * jnp.tanh(x * (GELU_C0 + GELU_C1 * (x * x)))


def _silu(x):
    h = 0.5 * x
    return h + h * jnp.tanh(h)


def _s5_prep_kernel(are_ref, aim_ref, ldt_ref, bre_ref, bim_ref, cre_ref, cim_ref,
                    taps_ref, ere_ref, eim_ref, fre_ref, fim_ref, pwre_ref, pwim_ref):
    a_re = are_ref[...]
    a_im = aim_ref[...]
    dt = jnp.exp(ldt_ref[...])
    mag = jnp.exp(a_re * dt)
    ab_re = mag * jnp.cos(a_im * dt)
    ab_im = mag * jnp.sin(a_im * dt)
    den = a_re * a_re + a_im * a_im
    nr = ab_re - 1.0
    ni = ab_im
    coef_re = (nr * a_re + ni * a_im) / den
    coef_im = (ni * a_re - nr * a_im) / den
    b_re = bre_ref[...]
    b_im = bim_ref[...]
    bb_re = coef_re * b_re - coef_im * b_im
    bb_im = coef_re * b_im + coef_im * b_re
    c_re = cre_ref[...]
    c_im = cim_ref[...]

    p_re = jnp.ones_like(ab_re)
    p_im = jnp.zeros_like(ab_re)
    ca_re = []
    ca_im = []
    for tau in range(S5_BLOCK + 1):
        ca_re.append(c_re * p_re - c_im * p_im)
        ca_im.append(c_re * p_im + c_im * p_re)
        if tau < S5_BLOCK:
            s = S5_BLOCK - 1 - tau
            ere_ref[s] = p_re * bb_re - p_im * bb_im
            eim_ref[s] = p_re * bb_im + p_im * bb_re
        if tau >= 1:
            fre_ref[tau - 1] = ca_re[tau]
            fim_ref[tau - 1] = -ca_im[tau]
        if tau == S5_BLOCK:
            pwre_ref[...] = p_re
            pwim_ref[...] = p_im
        p_re, p_im = (p_re * ab_re - p_im * ab_im, p_re * ab_im + p_im * ab_re)

    car = jnp.concatenate(ca_re[:S5_BLOCK], axis=1)
    cai = jnp.concatenate(ca_im[:S5_BLOCK], axis=1)
    dn = (((2,), (2,)), ((0,), (0,)))
    taps = (lax.dot_general(bb_re, car, dn, precision=lax.Precision.HIGHEST,
                            preferred_element_type=F32)
            - lax.dot_general(bb_im, cai, dn, precision=lax.Precision.HIGHEST,
                              preferred_element_type=F32))
    taps_ref[...] = taps


def _s5_prep(a_re, a_im, log_dt, b_re, b_im, c_re, c_im):
    g = S5_N_GROUPS
    bt_re = jnp.transpose(b_re, (0, 2, 1))
    bt_im = jnp.transpose(b_im, (0, 2, 1))
    tab = jax.ShapeDtypeStruct((S5_BLOCK, g, S5_GROUP, S5_STATE), F32)
    return pl.pallas_call(
        _s5_prep_kernel,
        out_shape=(jax.ShapeDtypeStruct((g, S5_GROUP, S5_BLOCK_W), F32), tab, tab, tab, tab,
                   jax.ShapeDtypeStruct((g, 1, S5_STATE), F32),
                   jax.ShapeDtypeStruct((g, 1, S5_STATE), F32)),
        name="s5_prep",
    )(a_re.reshape(g, 1, S5_STATE), a_im.reshape(g, 1, S5_STATE), log_dt.reshape(g, 1, 1),
      bt_re, bt_im, c_re, c_im)


def _in_proj_kernel(x_ref, g_ref, wz_ref, wxbct_ref, wdt_ref, wu_ref, wg_ref,
                    zs_ref, xbct_ref, dt_ref, dtt_ref, u_ref, gate_ref):
    h = (_rms(x_ref[...]) * g_ref[...]).astype(BF16)
    xbct_ref[...] = lax.dot_general(wxbct_ref[...], h, (((1,), (1,)), ((), ())),
                                    preferred_element_type=F32).astype(BF16)
    zs_ref[...] = _silu(jnp.dot(h, wz_ref[...], preferred_element_type=F32)).astype(BF16)
    u = jnp.dot(h, wu_ref[...], preferred_element_type=F32)
    for j in range(u_ref.shape[0]):
        u_ref[j] = u[:, j * V7X_LANES:(j + 1) * V7X_LANES]
    gate_ref[...] = jax.nn.sigmoid(
        jnp.dot(h, wg_ref[...], preferred_element_type=F32)).astype(BF16)
    dt = jnp.dot(h, wdt_ref[...], preferred_element_type=F32)
    dt_ref[...] = dt[:, :SSD_N_HEADS]
    for c in range(dtt_ref.shape[0]):
        dtt_ref[c] = dt[c * SSD_CHUNK:(c + 1) * SSD_CHUNK, :].T[:SSD_N_HEADS, :]


def _in_proj(x2, g_mix, w_in):
    t = x2.shape[0]
    tm = IN_PROJ_ROW_TILE
    wz = w_in[:, OFF_Z:OFF_XBC].astype(BF16)
    wxbct = w_in[:, OFF_XBC:OFF_DT].T.astype(BF16)
    wdt_pad = jnp.pad(w_in[:, OFF_DT:OFF_U],
                      ((0, 0), (0, V7X_LANES - SSD_N_HEADS))).astype(BF16)
    wu = w_in[:, OFF_U:OFF_G].astype(BF16)
    wg = w_in[:, OFF_G:].astype(BF16)
    n_slab = S5_WIDTH // V7X_LANES
    row = lambda w: pl.BlockSpec((tm, w), lambda i: (i, 0))
    return pl.pallas_call(
        _in_proj_kernel,
        grid=(t // tm,),
        in_specs=[row(D_MODEL), _const_spec((1, D_MODEL)), _const_spec(wz.shape),
                  _const_spec(wxbct.shape), _const_spec(wdt_pad.shape),
                  _const_spec(wu.shape), _const_spec(wg.shape)],
        out_specs=[row(SSD_D_INNER),
                   pl.BlockSpec((SSD_CONV_DIM, tm), lambda i: (0, i)),
                   row(SSD_N_HEADS),
                   pl.BlockSpec((tm // SSD_CHUNK, SSD_N_HEADS, SSD_CHUNK), lambda i: (i, 0, 0)),
                   pl.BlockSpec((n_slab, tm, V7X_LANES), lambda i: (0, i, 0)),
                   row(N_BRANCHES * D_MODEL)],
        out_shape=(jax.ShapeDtypeStruct((t, SSD_D_INNER), BF16),
                   jax.ShapeDtypeStruct((SSD_CONV_DIM, t), BF16),
                   jax.ShapeDtypeStruct((t, SSD_N_HEADS), F32),
                   jax.ShapeDtypeStruct((t // SSD_CHUNK, SSD_N_HEADS, SSD_CHUNK), F32),
                   jax.ShapeDtypeStruct((n_slab, t, V7X_LANES), F32),
                   jax.ShapeDtypeStruct((t, N_BRANCHES * D_MODEL), BF16)),
        compiler_params=pltpu.CompilerParams(dimension_semantics=("arbitrary",),
                                             vmem_limit_bytes=V7X_VMEM_LIMIT_BYTES),
        name="in_proj",
    )(x2, g_mix.reshape(1, D_MODEL), wz, wxbct, wdt_pad, wu, wg)


def _ssd_kernel(xbct_ref, zs_ref, dt_ref, dtt_ref, cwl_ref, cbl_ref, dtbc_ref, dtbr_ref,
                ac_ref, ar_ref, dcol_ref, ng_ref, y_ref, xprev, ht_s, st_s):
    tm = xbct_ref.shape[1]
    nch = tm // SSD_CHUNK
    q = SSD_CHUNK

    @pl.when(pl.program_id(1) == 0)
    def _():
        st_s[...] = jnp.zeros_like(st_s)
        xprev[...] = jnp.zeros_like(xprev)

    sh_r = lax.broadcasted_iota(jnp.int32, (2 * q, q), 0)
    sh_c = lax.broadcasted_iota(jnp.int32, (2 * q, q), 1)
    shift_all = jnp.concatenate(
        [(sh_r == sh_c + (q - (SSD_CONV - 1) + k)).astype(BF16) for k in range(SSD_CONV - 1)],
        axis=1)

    def conv_t(c, c0, n):
        if c == 0:
            win = jnp.concatenate([xprev[c0:c0 + n, :], xbct_ref[c0:c0 + n, 0:q]], axis=1)
        else:
            win = xbct_ref[c0:c0 + n, (c - 1) * q:(c + 1) * q]
        sh = jnp.dot(win, shift_all, preferred_element_type=F32)
        acc = cbl_ref[c0:c0 + n, :] + cwl_ref[SSD_CONV - 1, c0:c0 + n, :] * win[:, q:].astype(F32)
        for k in range(SSD_CONV - 1):
            acc = acc + cwl_ref[k, c0:c0 + n, :] * sh[:, k * q:(k + 1) * q]
        return acc + acc * jnp.tanh(acc)

    row_i = lax.broadcasted_iota(jnp.int32, (q, q), 0)
    col_i = lax.broadcasted_iota(jnp.int32, (q, q), 1)
    causal = row_i >= col_i
    tri = causal.astype(BF16)
    tri_t = (row_i <= col_i).astype(BF16)
    head_of_lane = lax.broadcasted_iota(jnp.int32, (q, SSD_GROUP_W), 1) // SSD_HEADDIM
    widen = (lax.broadcasted_iota(jnp.int32, (SSD_N_HEADS, SSD_D_INNER), 1) // SSD_HEADDIM
             == lax.broadcasted_iota(jnp.int32, (SSD_N_HEADS, SSD_D_INNER), 0)).astype(BF16)
    a_c = -jnp.exp(ac_ref[...])
    a_r = -jnp.exp(ar_ref[...])

    def chunk(c):
        r0 = c * q
        for c0 in range(0, SSD_CONV_DIM, CONV_ROWS):
            ht_s[c0:c0 + CONV_ROWS, :] = conv_t(c, c0, CONV_ROWS)
        dt_c = _softplus(dt_ref[pl.ds(r0, q), :] + dtbc_ref[...])
        dt_r = _softplus(dtt_ref[c] + dtbr_ref[...])
        la_c = dt_c * a_c
        la_r = dt_r * a_r
        cum_c = sum(jnp.dot(tri, p, preferred_element_type=F32) for p in _split3(la_c))
        cum_r = sum(jnp.dot(p, tri_t, preferred_element_type=F32) for p in _split3(la_r))
        to_end = dt_c * jnp.exp(cum_c[q - 1:q, :] - cum_c)
        dt_x = jnp.dot(dt_c.astype(BF16), widen, preferred_element_type=F32)
        te_x = jnp.dot(to_end.astype(BF16), widen, preferred_element_type=F32)
        ec_x = jnp.dot(jnp.exp(cum_c).astype(BF16), widen, preferred_element_type=F32)

        for g in range(SSD_N_GROUPS):
            b0 = SSD_D_INNER + g * SSD_D_STATE
            c0 = SSD_D_INNER + SSD_BC_W + g * SSD_D_STATE
            bgt = ht_s[b0:b0 + SSD_D_STATE, :].astype(BF16)
            cg = ht_s[c0:c0 + SSD_D_STATE, :].T.astype(BF16)
            cb = jnp.dot(cg, bgt, preferred_element_type=F32)
            ws = []
            for r in range(SSD_HPG):
                h = g * SSD_HPG + r
                cc = jnp.broadcast_to(cum_c[:, h:h + 1], (q, q))
                cr = jnp.broadcast_to(cum_r[h:h + 1, :], (q, q))
                lm = jnp.exp(jnp.where(causal, cc - cr, -jnp.inf))
                ws.append((cb * lm).astype(BF16))
            wcat = jnp.concatenate(ws, axis=1)
            cols = slice(g * SSD_GROUP_W, (g + 1) * SSD_GROUP_W)
            xs_g = ht_s[g * SSD_GROUP_W:(g + 1) * SSD_GROUP_W, :].T
            xdt_b = (xs_g * dt_x[:, cols]).astype(BF16)
            xdtd = (xs_g * te_x[:, cols]).astype(BF16)
            xbd = jnp.concatenate(
                [jnp.where(head_of_lane == r, xdt_b, jnp.zeros_like(xdt_b))
                 for r in range(SSD_HPG)], axis=0)
            y = jnp.dot(wcat, xbd, preferred_element_type=F32)
            prev = st_s[:, cols]
            y = y + jnp.dot(cg, prev.astype(BF16), preferred_element_type=F32) * ec_x[:, cols]
            st_s[:, cols] = prev * ec_x[q - 1:q, cols] + jnp.dot(
                bgt, xdtd, preferred_element_type=F32)
            y = y + dcol_ref[:, cols] * xs_g
            y = y * zs_ref[pl.ds(r0, q), cols].astype(F32)
            y_ref[pl.ds(r0, q), cols] = (_rms(y) * ng_ref[:, cols]).astype(BF16)

    for c in range(nch):
        chunk(c)
    xprev[...] = xbct_ref[:, tm - q:tm]


def _ssd(xbct, zs, dt, dtt, conv_w, conv_b, dt_bias, a_log, d_ssd, ssd_norm_g, bsz, seqlen):
    tm = SSD_ROW_TILE
    nt = seqlen // tm
    nch = tm // SSD_CHUNK
    h = SSD_N_HEADS
    row = lambda w: pl.BlockSpec((tm, w), lambda b, i: (b * nt + i, 0))
    cwl = jnp.broadcast_to((0.5 * conv_w)[:, :, None], (SSD_CONV, SSD_CONV_DIM, V7X_LANES))
    cbl = jnp.broadcast_to((0.5 * conv_b)[:, None], (SSD_CONV_DIM, V7X_LANES))
    return pl.pallas_call(
        _ssd_kernel,
        grid=(bsz, nt),
        in_specs=[pl.BlockSpec((SSD_CONV_DIM, tm), lambda b, i: (0, b * nt + i)),
                  row(SSD_D_INNER), row(h),
                  pl.BlockSpec((nch, h, SSD_CHUNK), lambda b, i: (b * nt + i, 0, 0)),
                  _const_spec(cwl.shape), _const_spec(cbl.shape),
                  _const_spec((1, h)), _const_spec((h, 1)), _const_spec((1, h)),
                  _const_spec((h, 1)), _const_spec((1, SSD_D_INNER)),
                  _const_spec((1, SSD_D_INNER))],
        out_specs=row(SSD_D_INNER),
        out_shape=jax.ShapeDtypeStruct((bsz * seqlen, SSD_D_INNER), BF16),
        scratch_shapes=[pltpu.VMEM((SSD_CONV_DIM, SSD_CHUNK), BF16),
                        pltpu.VMEM((SSD_CONV_DIM, SSD_CHUNK), F32),
                        pltpu.VMEM((SSD_D_STATE, SSD_D_INNER), F32)],
        compiler_params=pltpu.CompilerParams(dimension_semantics=("arbitrary", "arbitrary"),
                                             vmem_limit_bytes=V7X_VMEM_LIMIT_BYTES),
        name="ssd",
    )(xbct, zs, dt, dtt, cwl, cbl, dt_bias.reshape(1, h), dt_bias.reshape(h, 1),
      a_log.reshape(1, h), a_log.reshape(h, 1),
      jnp.repeat(d_ssd, SSD_HEADDIM).reshape(1, SSD_D_INNER), ssd_norm_g.reshape(1, -1))


def _s5_kernel(u_ref, mt_ref, et_ref, ft_ref, pre_ref, pim_ref, d_ref, y_ref,
               ut_s, yt_s, sre, sim, cre, cim):
    n_slab = u_ref.shape[0]
    nblk = u_ref.shape[1] // S5_BLOCK
    npair = S5_N_GROUPS // 2
    gps = V7X_LANES // S5_GROUP
    hw = V7X_LANES

    @pl.when(pl.program_id(1) == 0)
    def _():
        cre[...] = jnp.zeros_like(cre)
        cim[...] = jnp.zeros_like(cim)

    for s in range(S5_BLOCK):
        for j in range(n_slab):
            xt = u_ref[j, pl.ds(s, nblk, stride=S5_BLOCK), :].astype(BF16).T
            for gl in range(gps):
                ut_s[j * gps + gl, s * S5_GROUP:(s + 1) * S5_GROUP, :] = (
                    xt[gl * S5_GROUP:(gl + 1) * S5_GROUP, :])

    for pr in range(npair):
        ucat = jnp.concatenate([ut_s[2 * pr], ut_s[2 * pr + 1]], axis=0)
        sl = jnp.dot(et_ref[pr], ucat, preferred_element_type=F32).T
        sre[pl.ds(pr, nblk, stride=S5_STATE_PITCH), :] = sl[:, :hw]
        sim[pl.ds(pr, nblk, stride=S5_STATE_PITCH), :] = sl[:, hw:]

    a_re = pre_ref[...]
    a_im = pim_ref[...]

    def step(r, carry):
        c_re, c_im = carry
        rows = pl.ds(pl.multiple_of(r * S5_STATE_PITCH, V7X_SUBLANES), npair)
        l_re = sre[rows, :]
        l_im = sim[rows, :]
        sre[rows, :] = c_re
        sim[rows, :] = c_im
        return (a_re * c_re - a_im * c_im + l_re, a_re * c_im + a_im * c_re + l_im)

    c_re, c_im = lax.fori_loop(0, nblk, step, (cre[...], cim[...]))
    cre[...] = c_re
    cim[...] = c_im

    for pr in range(npair):
        sp = jnp.concatenate([sre[pl.ds(pr, nblk, stride=S5_STATE_PITCH), :],
                              sim[pl.ds(pr, nblk, stride=S5_STATE_PITCH), :]], axis=1).astype(BF16)
        cross = lax.dot_general(ft_ref[pr], sp, (((1,), (1,)), ((), ())),
                                preferred_element_type=F32)
        for j in range(2):
            gi = 2 * pr + j
            ug = ut_s[gi]
            y = (jnp.dot(mt_ref[gi], ug, preferred_element_type=F32)
                 + cross[j * S5_BLOCK_W:(j + 1) * S5_BLOCK_W, :]
                 + d_ref[gi] * ug.astype(F32))
            yt_s[gi] = _gelu_tanh(y).astype(BF16)

    for t in range(S5_BLOCK):
        for j in range(n_slab):
            zt = jnp.concatenate(
                [yt_s[j * gps + gl, t * S5_GROUP:(t + 1) * S5_GROUP, :] for gl in range(gps)],
                axis=0)
            y_ref[j, pl.ds(t, nblk, stride=S5_BLOCK), :] = zt.T.astype(F32)


def _s5(u_slab, mt, et, ft, pw_re, pw_im, d_col, bsz):
    n_slab, t, _ = u_slab.shape
    nblk = S5_BLOCKS_PER_STEP
    rows = nblk * S5_BLOCK
    parts = t // bsz // rows
    g, w = S5_N_GROUPS, S5_BLOCK_W
    io = pl.BlockSpec((n_slab, rows, V7X_LANES), lambda b, p: (0, b * parts + p, 0))
    state = pltpu.VMEM((nblk * S5_STATE_PITCH, V7X_LANES), F32)
    carry = pltpu.VMEM((g // 2, V7X_LANES), F32)
    return pl.pallas_call(
        _s5_kernel,
        grid=(bsz, parts),
        in_specs=[io, _const_spec(mt.shape), _const_spec(et.shape), _const_spec(ft.shape),
                  _const_spec(pw_re.shape), _const_spec(pw_im.shape), _const_spec(d_col.shape)],
        out_specs=io,
        out_shape=jax.ShapeDtypeStruct(u_slab.shape, F32),
        scratch_shapes=[pltpu.VMEM((g, w, nblk), BF16), pltpu.VMEM((g, w, nblk), BF16),
                        state, state, carry, carry],
        compiler_params=pltpu.CompilerParams(dimension_semantics=("arbitrary", "arbitrary"),
                                             vmem_limit_bytes=V7X_VMEM_LIMIT_BYTES),
        name="s5",
    )(u_slab, mt, et, ft, pw_re, pw_im, d_col)


def _s5_tables(taps, e_re, e_im, f_re, f_im, pw_re, pw_im, s5_d):
    g, q, w = S5_N_GROUPS, S5_BLOCK, S5_BLOCK_W
    m = jnp.stack([jnp.pad(taps[:, :, :w - S5_GROUP * s], ((0, 0), (0, 0), (S5_GROUP * s, 0)))
                   for s in range(q)], axis=1).reshape(g, w, w)
    eg_re = jnp.transpose(e_re, (1, 0, 2, 3)).reshape(g // 2, 2, w, S5_STATE)
    eg_im = jnp.transpose(e_im, (1, 0, 2, 3)).reshape(g // 2, 2, w, S5_STATE)
    zero = jnp.zeros_like(eg_re[:, 0])
    e_top = jnp.concatenate([eg_re[:, 0], zero, eg_im[:, 0], zero], axis=-1)
    e_bot = jnp.concatenate([zero, eg_re[:, 1], zero, eg_im[:, 1]], axis=-1)
    e = jnp.concatenate([e_top, e_bot], axis=1)
    fg_re = jnp.transpose(f_re, (1, 3, 0, 2)).reshape(g // 2, 2, S5_STATE, w)
    fg_im = jnp.transpose(f_im, (1, 3, 0, 2)).reshape(g // 2, 2, S5_STATE, w)
    zf = jnp.zeros_like(fg_re[:, 0])
    f = jnp.concatenate([
        jnp.concatenate([fg_re[:, 0], zf], axis=-1),
        jnp.concatenate([zf, fg_re[:, 1]], axis=-1),
        jnp.concatenate([fg_im[:, 0], zf], axis=-1),
        jnp.concatenate([zf, fg_im[:, 1]], axis=-1)], axis=1)
    tr = lambda a: jnp.swapaxes(a, 1, 2).astype(BF16)
    pw_re = pw_re.reshape(g // 2, 2 * S5_STATE)
    pw_im = pw_im.reshape(g // 2, 2 * S5_STATE)
    d_col = jnp.tile(s5_d.reshape(g, 1, S5_GROUP), (1, q, 1)).reshape(g, w, 1)
    return tr(m), tr(e), tr(f), pw_re, pw_im, d_col


def _merge_mlp_kernel(x_ref, ya_ref, yb_ref, gate_ref, gw_ref, gb_ref, wa_ref, wb_ref, wo_ref,
                      g2_ref, wi_ref, wo2_ref, g3_ref, o_ref):
    ybf = jnp.concatenate([yb_ref[j] for j in range(yb_ref.shape[0])], axis=1)
    yb = ybf.astype(BF16)
    glu = jnp.dot(yb, gw_ref[...], preferred_element_type=F32) + gb_ref[...]
    ybg = (ybf * jax.nn.sigmoid(glu)).astype(BF16)
    pa = jnp.dot(ya_ref[...], wa_ref[...], preferred_element_type=F32)
    pb = jnp.dot(ybg, wb_ref[...], preferred_element_type=F32)
    merged = (gate_ref[:, :D_MODEL].astype(F32) * pa
              + gate_ref[:, D_MODEL:].astype(F32) * pb).astype(BF16)
    x1 = x_ref[...] + jnp.dot(merged, wo_ref[...], preferred_element_type=F32)
    h2 = (_rms(x1) * g2_ref[...]).astype(BF16)
    acc = x1
    for k in range(D_FF // D_MODEL):
        cols = slice(k * D_MODEL, (k + 1) * D_MODEL)
        hid = jnp.dot(h2, wi_ref[:, cols], preferred_element_type=F32)
        hid = jnp.square(jnp.maximum(hid, 0.0)).astype(BF16)
        acc = acc + jnp.dot(hid, wo2_ref[cols, :], preferred_element_type=F32)
    o_ref[...] = _rms(acc) * g3_ref[...]


def _merge_mlp(x2, ya, yb_slab, gates, glu_w, glu_b, w_branch, w_out, g_mlp, w_mlp_in,
               w_mlp_out, g_final):
    t = x2.shape[0]
    tm = ROW_TILE
    wa = w_branch[:SSD_D_INNER].astype(BF16)
    wb = w_branch[SSD_D_INNER:].astype(BF16)
    row = lambda w: pl.BlockSpec((tm, w), lambda i: (i, 0))
    vec = lambda v: v.reshape(1, -1)
    return pl.pallas_call(
        _merge_mlp_kernel,
        grid=(t // tm,),
        in_specs=[row(D_MODEL), row(SSD_D_INNER),
                  pl.BlockSpec((yb_slab.shape[0], tm, V7X_LANES), lambda i: (0, i, 0)),
                  row(N_BRANCHES * D_MODEL),
                  _const_spec((S5_WIDTH, S5_WIDTH)), _const_spec((1, S5_WIDTH)),
                  _const_spec(wa.shape), _const_spec(wb.shape), _const_spec((D_MODEL, D_MODEL)),
                  _const_spec((1, D_MODEL)), _const_spec((D_MODEL, D_FF)),
                  _const_spec((D_FF, D_MODEL)), _const_spec((1, D_MODEL))],
        out_specs=row(D_MODEL),
        out_shape=jax.ShapeDtypeStruct((t, D_MODEL), F32),
        compiler_params=pltpu.CompilerParams(dimension_semantics=("arbitrary",),
                                             vmem_limit_bytes=V7X_VMEM_LIMIT_BYTES),
        name="merge_mlp",
    )(x2, ya, yb_slab, gates, glu_w.astype(BF16), vec(glu_b), wa, wb, w_out.astype(BF16),
      vec(g_mlp), w_mlp_in.astype(BF16), w_mlp_out.astype(BF16), vec(g_final))


def kernel(x, norm_mix_g, w_in, conv_w, conv_b, dt_bias, a_log, d_ssd, ssd_norm_g, s5_a_re, s5_a_im, s5_log_dt, s5_b_re, s5_b_im, s5_c_re, s5_c_im, s5_d, s5_glu_w, s5_glu_b, w_branch, w_out, norm_mlp_g, w_mlp_in, w_mlp_out, norm_final_g):
    bsz, seqlen, _ = x.shape
    t = bsz * seqlen
    assert seqlen % SSD_ROW_TILE == 0 and t % ROW_TILE == 0 and t % IN_PROJ_ROW_TILE == 0
    assert seqlen % (S5_BLOCKS_PER_STEP * S5_BLOCK) == 0
    x2 = x.reshape(t, D_MODEL)

    zs, xbct, dt, dtt, u_slab, gates = _in_proj(x2, norm_mix_g, w_in)
    ya = _ssd(xbct, zs, dt, dtt, conv_w, conv_b, dt_bias, a_log, d_ssd, ssd_norm_g, bsz,
              seqlen)
    tables = _s5_tables(*_s5_prep(s5_a_re, s5_a_im, s5_log_dt, s5_b_re, s5_b_im,
                                  s5_c_re, s5_c_im), s5_d)
    yb_slab = _s5(u_slab, *tables, bsz)
    out = _merge_mlp(x2, ya, yb_slab, gates, s5_glu_w, s5_glu_b, w_branch, w_out, norm_mlp_g,
                     w_mlp_in, w_mlp_out, norm_final_g)
    return out.reshape(bsz, seqlen, D_MODEL)
```

```python
import jax
import jax.numpy as jnp
from jax import lax
from jax.experimental import pallas as pl
from jax.experimental.pallas import tpu as pltpu

F32 = jnp.float32
BF16 = jnp.bfloat16

D_MODEL = 1024
SSD_D_INNER = D_MODEL
SSD_HEADDIM = 64
SSD_N_HEADS = SSD_D_INNER // SSD_HEADDIM
SSD_N_GROUPS = 4
SSD_HPG = SSD_N_HEADS // SSD_N_GROUPS
SSD_D_STATE = 128
SSD_CONV = 4
SSD_CHUNK = 128
SSD_GROUP_W = SSD_D_INNER // SSD_N_GROUPS
SSD_BC_W = SSD_N_GROUPS * SSD_D_STATE
SSD_CONV_DIM = SSD_D_INNER + 2 * SSD_BC_W
S5_WIDTH = D_MODEL // 2
S5_GROUP = 16
S5_N_GROUPS = S5_WIDTH // S5_GROUP
S5_STATE = 64
S5_BLOCK = 16
S5_BLOCK_W = S5_BLOCK * S5_GROUP
D_FF = 4 * D_MODEL
N_BRANCHES = 2
OFF_Z = 0
OFF_XBC = OFF_Z + SSD_D_INNER
OFF_DT = OFF_XBC + SSD_CONV_DIM
OFF_U = OFF_DT + SSD_N_HEADS
OFF_G = OFF_U + S5_WIDTH
EPS = 1e-6

V7X_LANES = 128
V7X_SUBLANES = 8
V7X_VMEM_LIMIT_BYTES = 56 * 1024 * 1024

ROW_TILE = 512
IN_PROJ_ROW_TILE = 1024
SSD_ROW_TILE = 512
S5_BLOCKS_PER_STEP = 128
CONV_ROWS = 256
S5_STATE_PITCH = 24


def _const_spec(shape):
    zeros = (0,) * len(shape)
    return pl.BlockSpec(shape, lambda *_: zeros, pipeline_mode=pl.Buffered(1))


def _rms(x):
    return x * lax.rsqrt(jnp.mean(x * x, axis=-1, keepdims=True) + EPS)


def _split3(x):
    hi = x.astype(BF16)
    r1 = x - hi.astype(F32)
    mid = r1.astype(BF16)
    lo = (r1 - mid.astype(F32)).astype(BF16)
    return hi, mid, lo


def _softplus(x):
    return jnp.maximum(x, 0.0) + jnp.log(1.0 + jnp.exp(-jnp.abs(x)))


GELU_C0 = 0.7978845608028654
GELU_C1 = GELU_C0 * 0.044715


def _gelu_tanh(x):
    h = 0.5 * x
    return h + h * jnp.tanh(x * (GELU_C0 + GELU_C1 * (x * x)))


def _silu(x):
    h = 0.5 * x
    return h + h * jnp.tanh(h)


def _s5_prep_kernel(are_ref, aim_ref, ldt_ref, bre_ref, bim_ref, cre_ref, cim_ref,
                    mt_ref, ere_ref, eim_ref, fre_ref, fim_ref, pwre_ref, pwim_ref):
    a_re = are_ref[...]
    a_im = aim_ref[...]
    dt = jnp.exp(ldt_ref[...])
    mag = jnp.exp(a_re * dt)
    ab_re = mag * jnp.cos(a_im * dt)
    ab_im = mag * jnp.sin(a_im * dt)
    den = a_re * a_re + a_im * a_im
    nr = ab_re - 1.0
    ni = ab_im
    coef_re = (nr * a_re + ni * a_im) / den
    coef_im = (ni * a_re - nr * a_im) / den
    b_re = bre_ref[...]
    b_im = bim_ref[...]
    bb_re = coef_re * b_re - coef_im * b_im
    bb_im = coef_re * b_im + coef_im * b_re
    c_re = cre_ref[...]
    c_im = cim_ref[...]

    p_re = jnp.ones_like(ab_re)
    p_im = jnp.zeros_like(ab_re)
    pb_re = [None] * S5_BLOCK
    pb_im = [None] * S5_BLOCK
    for tau in range(S5_BLOCK + 1):
        if tau < S5_BLOCK:
            s = S5_BLOCK - 1 - tau
            pb_re[s] = p_re * bb_re - p_im * bb_im
            pb_im[s] = p_re * bb_im + p_im * bb_re
            ere_ref[s] = pb_re[s]
            eim_ref[s] = pb_im[s]
        if tau >= 1:
            fre_ref[tau - 1] = c_re * p_re - c_im * p_im
            fim_ref[tau - 1] = -(c_re * p_im + c_im * p_re)
        if tau == S5_BLOCK:
            pwre_ref[...] = p_re
            pwim_ref[...] = p_im
        p_re, p_im = (p_re * ab_re - p_im * ab_im, p_re * ab_im + p_im * ab_re)

    dn = (((2,), (2,)), ((0,), (0,)))
    krev = (lax.dot_general(c_re, jnp.concatenate(pb_re, axis=1), dn,
                            precision=lax.Precision.HIGHEST, preferred_element_type=F32)
            - lax.dot_general(c_im, jnp.concatenate(pb_im, axis=1), dn,
                              precision=lax.Precision.HIGHEST, preferred_element_type=F32))
    lane = lax.broadcasted_iota(jnp.int32, krev.shape, 2)
    for t in range(S5_BLOCK):
        keep = S5_GROUP * (t + 1)
        moved = pltpu.roll(krev, keep % S5_BLOCK_W, 2)
        mt_ref[:, t * S5_GROUP:(t + 1) * S5_GROUP, :] = jnp.where(
            lane < keep, moved, 0.0).astype(BF16)


def _s5_prep(a_re, a_im, log_dt, b_re, b_im, c_re, c_im):
    g = S5_N_GROUPS
    bt_re = jnp.transpose(b_re, (0, 2, 1))
    bt_im = jnp.transpose(b_im, (0, 2, 1))
    tab = jax.ShapeDtypeStruct((S5_BLOCK, g, S5_GROUP, S5_STATE), F32)
    return pl.pallas_call(
        _s5_prep_kernel,
        out_shape=(jax.ShapeDtypeStruct((g, S5_BLOCK_W, S5_BLOCK_W), BF16), tab, tab, tab, tab,
                   jax.ShapeDtypeStruct((g, 1, S5_STATE), F32),
                   jax.ShapeDtypeStruct((g, 1, S5_STATE), F32)),
        name="s5_prep",
    )(a_re.reshape(g, 1, S5_STATE), a_im.reshape(g, 1, S5_STATE), log_dt.reshape(g, 1, 1),
      bt_re, bt_im, c_re, c_im)


def _in_proj_kernel(x_ref, g_ref, wz_ref, wxbct_ref, wdt_ref, wu_ref, wg_ref,
                    zs_ref, xbct_ref, dt_ref, dtt_ref, u_ref, gate_ref):
    h = (_rms(x_ref[...]) * g_ref[...]).astype(BF16)
    xbct_ref[...] = lax.dot_general(wxbct_ref[...], h, (((1,), (1,)), ((), ())),
                                    preferred_element_type=F32).astype(BF16)
    zs_ref[...] = _silu(jnp.dot(h, wz_ref[...], preferred_element_type=F32)).astype(BF16)
    u = jnp.dot(h, wu_ref[...], preferred_element_type=F32)
    for j in range(u_ref.shape[0]):
        u_ref[j] = u[:, j * V7X_LANES:(j + 1) * V7X_LANES]
    gate_ref[...] = jax.nn.sigmoid(
        jnp.dot(h, wg_ref[...], preferred_element_type=F32)).astype(BF16)
    dt = jnp.dot(h, wdt_ref[...], preferred_element_type=F32)
    dt_ref[...] = dt[:, :SSD_N_HEADS]
    for c in range(dtt_ref.shape[0]):
        dtt_ref[c] = dt[c * SSD_CHUNK:(c + 1) * SSD_CHUNK, :].T[:SSD_N_HEADS, :]


def _in_proj(x2, g_mix, w_in):
    t = x2.shape[0]
    tm = IN_PROJ_ROW_TILE
    wz = w_in[:, OFF_Z:OFF_XBC].astype(BF16)
    wxbct = w_in[:, OFF_XBC:OFF_DT].T.astype(BF16)
    wdt_pad = jnp.pad(w_in[:, OFF_DT:OFF_U],
                      ((0, 0), (0, V7X_LANES - SSD_N_HEADS))).astype(BF16)
    wu = w_in[:, OFF_U:OFF_G].astype(BF16)
    wg = w_in[:, OFF_G:].astype(BF16)
    n_slab = S5_WIDTH // V7X_LANES
    row = lambda w: pl.BlockSpec((tm, w), lambda i: (i, 0))
    return pl.pallas_call(
        _in_proj_kernel,
        grid=(t // tm,),
        in_specs=[row(D_MODEL), _const_spec((1, D_MODEL)), _const_spec(wz.shape),
                  _const_spec(wxbct.shape), _const_spec(wdt_pad.shape),
                  _const_spec(wu.shape), _const_spec(wg.shape)],
        out_specs=[row(SSD_D_INNER),
                   pl.BlockSpec((SSD_CONV_DIM, tm), lambda i: (0, i)),
                   row(SSD_N_HEADS),
                   pl.BlockSpec((tm // SSD_CHUNK, SSD_N_HEADS, SSD_CHUNK), lambda i: (i, 0, 0)),
                   pl.BlockSpec((n_slab, tm, V7X_LANES), lambda i: (0, i, 0)),
                   row(N_BRANCHES * D_MODEL)],
        out_shape=(jax.ShapeDtypeStruct((t, SSD_D_INNER), BF16),
                   jax.ShapeDtypeStruct((SSD_CONV_DIM, t), BF16),
                   jax.ShapeDtypeStruct((t, SSD_N_HEADS), F32),
                   jax.ShapeDtypeStruct((t // SSD_CHUNK, SSD_N_HEADS, SSD_CHUNK), F32),
                   jax.ShapeDtypeStruct((n_slab, t, V7X_LANES), F32),
                   jax.ShapeDtypeStruct((t, N_BRANCHES * D_MODEL), BF16)),
        compiler_params=pltpu.CompilerParams(dimension_semantics=("arbitrary",),
                                             vmem_limit_bytes=V7X_VMEM_LIMIT_BYTES),
        name="in_proj",
    )(x2, g_mix.reshape(1, D_MODEL), wz, wxbct, wdt_pad, wu, wg)


def _ssd_kernel(xbct_ref, zs_ref, dt_ref, dtt_ref, cwl_ref, cbl_ref, dtbc_ref, dtbr_ref,
                ac_ref, ar_ref, dcol_ref, ng_ref, y_ref, xprev, ht_s, st_s):
    tm = xbct_ref.shape[1]
    nch = tm // SSD_CHUNK
    q = SSD_CHUNK

    @pl.when(pl.program_id(1) == 0)
    def _():
        st_s[...] = jnp.zeros_like(st_s)
        xprev[...] = jnp.zeros_like(xprev)

    sh_r = lax.broadcasted_iota(jnp.int32, (2 * q, q), 0)
    sh_c = lax.broadcasted_iota(jnp.int32, (2 * q, q), 1)
    shift_all = jnp.concatenate(
        [(sh_r == sh_c + (q - (SSD_CONV - 1) + k)).astype(BF16) for k in range(SSD_CONV - 1)],
        axis=1)

    def conv_t(c, c0, n):
        if c == 0:
            win = jnp.concatenate([xprev[c0:c0 + n, :], xbct_ref[c0:c0 + n, 0:q]], axis=1)
        else:
            win = xbct_ref[c0:c0 + n, (c - 1) * q:(c + 1) * q]
        sh = jnp.dot(win, shift_all, preferred_element_type=F32)
        acc = cbl_ref[c0:c0 + n, :] + cwl_ref[SSD_CONV - 1, c0:c0 + n, :] * win[:, q:].astype(F32)
        for k in range(SSD_CONV - 1):
            acc = acc + cwl_ref[k, c0:c0 + n, :] * sh[:, k * q:(k + 1) * q]
        return acc + acc * jnp.tanh(acc)

    row_i = lax.broadcasted_iota(jnp.int32, (q, q), 0)
    col_i = lax.broadcasted_iota(jnp.int32, (q, q), 1)
    causal = row_i >= col_i
    tri = causal.astype(BF16)
    tri_t = (row_i <= col_i).astype(BF16)
    head_of_lane = lax.broadcasted_iota(jnp.int32, (q, SSD_GROUP_W), 1) // SSD_HEADDIM
    widen = (lax.broadcasted_iota(jnp.int32, (SSD_N_HEADS, SSD_D_INNER), 1) // SSD_HEADDIM
             == lax.broadcasted_iota(jnp.int32, (SSD_N_HEADS, SSD_D_INNER), 0)).astype(BF16)
    a_c = -jnp.exp(ac_ref[...])
    a_r = -jnp.exp(ar_ref[...])

    def chunk(c):
        r0 = c * q
        for c0 in range(0, SSD_CONV_DIM, CONV_ROWS):
            ht_s[c0:c0 + CONV_ROWS, :] = conv_t(c, c0, CONV_ROWS)
        dt_c = _softplus(dt_ref[pl.ds(r0, q), :] + dtbc_ref[...])
        dt_r = _softplus(dtt_ref[c] + dtbr_ref[...])
        la_c = dt_c * a_c
        la_r = dt_r * a_r
        cum_c = sum(jnp.dot(tri, p, preferred_element_type=F32) for p in _split3(la_c))
        cum_r = sum(jnp.dot(p, tri_t, preferred_element_type=F32) for p in _split3(la_r))
        cum_rl = cum_r - jnp.log(dt_r)
        to_end = dt_c * jnp.exp(cum_c[q - 1:q, :] - cum_c)
        te_x = jnp.dot(to_end.astype(BF16), widen, preferred_element_type=F32)
        ec_x = jnp.dot(jnp.exp(cum_c).astype(BF16), widen, preferred_element_type=F32)

        for g in range(SSD_N_GROUPS):
            b0 = SSD_D_INNER + g * SSD_D_STATE
            c0 = SSD_D_INNER + SSD_BC_W + g * SSD_D_STATE
            bgt = ht_s[b0:b0 + SSD_D_STATE, :].astype(BF16)
            cg = ht_s[c0:c0 + SSD_D_STATE, :].T.astype(BF16)
            cb = jnp.dot(cg, bgt, preferred_element_type=F32)
            ws = []
            for r in range(SSD_HPG):
                h = g * SSD_HPG + r
                cc = jnp.broadcast_to(cum_c[:, h:h + 1], (q, q))
                cr = jnp.broadcast_to(cum_rl[h:h + 1, :], (q, q))
                lm = jnp.exp(jnp.where(causal, cc - cr, -jnp.inf))
                ws.append((cb * lm).astype(BF16))
            wcat = jnp.concatenate(ws, axis=1)
            cols = slice(g * SSD_GROUP_W, (g + 1) * SSD_GROUP_W)
            xs_g = ht_s[g * SSD_GROUP_W:(g + 1) * SSD_GROUP_W, :].T
            xs_b = xs_g.astype(BF16)
            xdtd = (xs_g * te_x[:, cols]).astype(BF16)
            xbd = jnp.concatenate(
                [jnp.where(head_of_lane == r, xs_b, jnp.zeros_like(xs_b))
                 for r in range(SSD_HPG)], axis=0)
            y = jnp.dot(wcat, xbd, preferred_element_type=F32)
            prev = st_s[:, cols]
            y = y + jnp.dot(cg, prev.astype(BF16), preferred_element_type=F32) * ec_x[:, cols]
            st_s[:, cols] = prev * ec_x[q - 1:q, cols] + jnp.dot(
                bgt, xdtd, preferred_element_type=F32)
            y = y + dcol_ref[:, cols] * xs_g
            y = y * zs_ref[pl.ds(r0, q), cols].astype(F32)
            y_ref[pl.ds(r0, q), cols] = (_rms(y) * ng_ref[:, cols]).astype(BF16)

    for c in range(nch):
        chunk(c)
    xprev[...] = xbct_ref[:, tm - q:tm]


def _ssd(xbct, zs, dt, dtt, conv_w, conv_b, dt_bias, a_log, d_ssd, ssd_norm_g, bsz, seqlen):
    tm = SSD_ROW_TILE
    nt = seqlen // tm
    nch = tm // SSD_CHUNK
    h = SSD_N_HEADS
    row = lambda w: pl.BlockSpec((tm, w), lambda b, i: (b * nt + i, 0))
    cwl = jnp.broadcast_to((0.5 * conv_w)[:, :, None], (SSD_CONV, SSD_CONV_DIM, V7X_LANES))
    cbl = jnp.broadcast_to((0.5 * conv_b)[:, None], (SSD_CONV_DIM, V7X_LANES))
    return pl.pallas_call(
        _ssd_kernel,
        grid=(bsz, nt),
        in_specs=[pl.BlockSpec((SSD_CONV_DIM, tm), lambda b, i: (0, b * nt + i)),
                  row(SSD_D_INNER), row(h),
                  pl.BlockSpec((nch, h, SSD_CHUNK), lambda b, i: (b * nt + i, 0, 0)),
                  _const_spec(cwl.shape), _const_spec(cbl.shape),
                  _const_spec((1, h)), _const_spec((h, 1)), _const_spec((1, h)),
                  _const_spec((h, 1)), _const_spec((1, SSD_D_INNER)),
                  _const_spec((1, SSD_D_INNER))],
        out_specs=row(SSD_D_INNER),
        out_shape=jax.ShapeDtypeStruct((bsz * seqlen, SSD_D_INNER), BF16),
        scratch_shapes=[pltpu.VMEM((SSD_CONV_DIM, SSD_CHUNK), BF16),
                        pltpu.VMEM((SSD_CONV_DIM, SSD_CHUNK), F32),
                        pltpu.VMEM((SSD_D_STATE, SSD_D_INNER), F32)],
        compiler_params=pltpu.CompilerParams(dimension_semantics=("arbitrary", "arbitrary"),
                                             vmem_limit_bytes=V7X_VMEM_LIMIT_BYTES),
        name="ssd",
    )(xbct, zs, dt, dtt, cwl, cbl, dt_bias.reshape(1, h), dt_bias.reshape(h, 1),
      a_log.reshape(1, h), a_log.reshape(h, 1),
      jnp.repeat(d_ssd, SSD_HEADDIM).reshape(1, SSD_D_INNER), ssd_norm_g.reshape(1, -1))


def _s5_kernel(u_ref, mt_ref, et_ref, ft_ref, pre_ref, pim_ref, d_ref, y_ref,
               ut_s, yt_s, sre, sim, cre, cim):
    n_slab = u_ref.shape[0]
    nblk = u_ref.shape[1] // S5_BLOCK
    npair = S5_N_GROUPS // 2
    gps = V7X_LANES // S5_GROUP
    hw = V7X_LANES

    @pl.when(pl.program_id(1) == 0)
    def _():
        cre[...] = jnp.zeros_like(cre)
        cim[...] = jnp.zeros_like(cim)

    for s in range(S5_BLOCK):
        for j in range(n_slab):
            xt = u_ref[j, pl.ds(s, nblk, stride=S5_BLOCK), :].astype(BF16).T
            for gl in range(gps):
                ut_s[j * gps + gl, s * S5_GROUP:(s + 1) * S5_GROUP, :] = (
                    xt[gl * S5_GROUP:(gl + 1) * S5_GROUP, :])

    for pr in range(npair):
        ucat = jnp.concatenate([ut_s[2 * pr], ut_s[2 * pr + 1]], axis=0)
        sl = jnp.dot(et_ref[pr], ucat, preferred_element_type=F32).T
        sre[pl.ds(pr, nblk, stride=S5_STATE_PITCH), :] = sl[:, :hw]
        sim[pl.ds(pr, nblk, stride=S5_STATE_PITCH), :] = sl[:, hw:]

    a_re = pre_ref[...]
    a_im = pim_ref[...]

    def step(r, carry):
        c_re, c_im = carry
        rows = pl.ds(pl.multiple_of(r * S5_STATE_PITCH, V7X_SUBLANES), npair)
        l_re = sre[rows, :]
        l_im = sim[rows, :]
        sre[rows, :] = c_re
        sim[rows, :] = c_im
        return (a_re * c_re - a_im * c_im + l_re, a_re * c_im + a_im * c_re + l_im)

    c_re, c_im = lax.fori_loop(0, nblk, step, (cre[...], cim[...]))
    cre[...] = c_re
    cim[...] = c_im

    for pr in range(npair):
        sp = jnp.concatenate([sre[pl.ds(pr, nblk, stride=S5_STATE_PITCH), :],
                              sim[pl.ds(pr, nblk, stride=S5_STATE_PITCH), :]], axis=1).astype(BF16)
        cross = lax.dot_general(ft_ref[pr], sp, (((1,), (1,)), ((), ())),
                                preferred_element_type=F32)
        for j in range(2):
            gi = 2 * pr + j
            ug = ut_s[gi]
            y = (jnp.dot(mt_ref[gi], ug, preferred_element_type=F32)
                 + cross[j * S5_BLOCK_W:(j + 1) * S5_BLOCK_W, :]
                 + d_ref[gi] * ug.astype(F32))
            yt_s[gi] = _gelu_tanh(y).astype(BF16)

    for t in range(S5_BLOCK):
        for j in range(n_slab):
            zt = jnp.concatenate(
                [yt_s[j * gps + gl, t * S5_GROUP:(t + 1) * S5_GROUP, :] for gl in range(gps)],
                axis=0)
            y_ref[j, pl.ds(t, nblk, stride=S5_BLOCK), :] = zt.T.astype(F32)


def _s5(u_slab, mt, et, ft, pw_re, pw_im, d_col, bsz):
    n_slab, t, _ = u_slab.shape
    nblk = S5_BLOCKS_PER_STEP
    rows = nblk * S5_BLOCK
    parts = t // bsz // rows
    g, w = S5_N_GROUPS, S5_BLOCK_W
    io = pl.BlockSpec((n_slab, rows, V7X_LANES), lambda b, p: (0, b * parts + p, 0))
    state = pltpu.VMEM((nblk * S5_STATE_PITCH, V7X_LANES), F32)
    carry = pltpu.VMEM((g // 2, V7X_LANES), F32)
    return pl.pallas_call(
        _s5_kernel,
        grid=(bsz, parts),
        in_specs=[io, _const_spec(mt.shape), _const_spec(et.shape), _const_spec(ft.shape),
                  _const_spec(pw_re.shape), _const_spec(pw_im.shape), _const_spec(d_col.shape)],
        out_specs=io,
        out_shape=jax.ShapeDtypeStruct(u_slab.shape, F32),
        scratch_shapes=[pltpu.VMEM((g, w, nblk), BF16), pltpu.VMEM((g, w, nblk), BF16),
                        state, state, carry, carry],
        compiler_params=pltpu.CompilerParams(dimension_semantics=("arbitrary", "arbitrary"),
                                             vmem_limit_bytes=V7X_VMEM_LIMIT_BYTES),
        name="s5",
    )(u_slab, mt, et, ft, pw_re, pw_im, d_col)


def _s5_tables(mt, e_re, e_im, f_re, f_im, pw_re, pw_im, s5_d):
    g, q, w = S5_N_GROUPS, S5_BLOCK, S5_BLOCK_W
    eg_re = jnp.transpose(e_re, (1, 0, 2, 3)).reshape(g // 2, 2, w, S5_STATE)
    eg_im = jnp.transpose(e_im, (1, 0, 2, 3)).reshape(g // 2, 2, w, S5_STATE)
    zero = jnp.zeros_like(eg_re[:, 0])
    e_top = jnp.concatenate([eg_re[:, 0], zero, eg_im[:, 0], zero], axis=-1)
    e_bot = jnp.concatenate([zero, eg_re[:, 1], zero, eg_im[:, 1]], axis=-1)
    e = jnp.concatenate([e_top, e_bot], axis=1)
    fg_re = jnp.transpose(f_re, (1, 3, 0, 2)).reshape(g // 2, 2, S5_STATE, w)
    fg_im = jnp.transpose(f_im, (1, 3, 0, 2)).reshape(g // 2, 2, S5_STATE, w)
    zf = jnp.zeros_like(fg_re[:, 0])
    f = jnp.concatenate([
        jnp.concatenate([fg_re[:, 0], zf], axis=-1),
        jnp.concatenate([zf, fg_re[:, 1]], axis=-1),
        jnp.concatenate([fg_im[:, 0], zf], axis=-1),
        jnp.concatenate([zf, fg_im[:, 1]], axis=-1)], axis=1)
    tr = lambda a: jnp.swapaxes(a, 1, 2).astype(BF16)
    pw_re = pw_re.reshape(g // 2, 2 * S5_STATE)
    pw_im = pw_im.reshape(g // 2, 2 * S5_STATE)
    d_col = jnp.tile(s5_d.reshape(g, 1, S5_GROUP), (1, q, 1)).reshape(g, w, 1)
    return mt, tr(e), tr(f), pw_re, pw_im, d_col


def _merge_mlp_kernel(x_ref, ya_ref, yb_ref, gate_ref, gw_ref, gb_ref, wa_ref, wb_ref, wo_ref,
                      g2_ref, wi_ref, wo2_ref, g3_ref, o_ref):
    ybf = jnp.concatenate([yb_ref[j] for j in range(yb_ref.shape[0])], axis=1)
    yb = ybf.astype(BF16)
    glu = jnp.dot(yb, gw_ref[...], preferred_element_type=F32) + gb_ref[...]
    ybg = (ybf * jax.nn.sigmoid(glu)).astype(BF16)
    pa = jnp.dot(ya_ref[...], wa_ref[...], preferred_element_type=F32)
    pb = jnp.dot(ybg, wb_ref[...], preferred_element_type=F32)
    merged = (gate_ref[:, :D_MODEL].astype(F32) * pa
              + gate_ref[:, D_MODEL:].astype(F32) * pb).astype(BF16)
    x1 = x_ref[...] + jnp.dot(merged, wo_ref[...], preferred_element_type=F32)
    h2 = (_rms(x1) * g2_ref[...]).astype(BF16)
    acc = x1
    for k in range(D_FF // D_MODEL):
        cols = slice(k * D_MODEL, (k + 1) * D_MODEL)
        hid = jnp.dot(h2, wi_ref[:, cols], preferred_element_type=F32)
        hid = jnp.square(jnp.maximum(hid, 0.0)).astype(BF16)
        acc = acc + jnp.dot(hid, wo2_ref[cols, :], preferred_element_type=F32)
    o_ref[...] = _rms(acc) * g3_ref[...]


def _merge_mlp(x2, ya, yb_slab, gates, glu_w, glu_b, w_branch, w_out, g_mlp, w_mlp_in,
               w_mlp_out, g_final):
    t = x2.shape[0]
    tm = ROW_TILE
    wa = w_branch[:SSD_D_INNER].astype(BF16)
    wb = w_branch[SSD_D_INNER:].astype(BF16)
    row = lambda w: pl.BlockSpec((tm, w), lambda i: (i, 0))
    vec = lambda v: v.reshape(1, -1)
    return pl.pallas_call(
        _merge_mlp_kernel,
        grid=(t // tm,),
        in_specs=[row(D_MODEL), row(SSD_D_INNER),
                  pl.BlockSpec((yb_slab.shape[0], tm, V7X_LANES), lambda i: (0, i, 0)),
                  row(N_BRANCHES * D_MODEL),
                  _const_spec((S5_WIDTH, S5_WIDTH)), _const_spec((1, S5_WIDTH)),
                  _const_spec(wa.shape), _const_spec(wb.shape), _const_spec((D_MODEL, D_MODEL)),
                  _const_spec((1, D_MODEL)), _const_spec((D_MODEL, D_FF)),
                  _const_spec((D_FF, D_MODEL)), _const_spec((1, D_MODEL))],
        out_specs=row(D_MODEL),
        out_shape=jax.ShapeDtypeStruct((t, D_MODEL), F32),
        compiler_params=pltpu.CompilerParams(dimension_semantics=("arbitrary",),
                                             vmem_limit_bytes=V7X_VMEM_LIMIT_BYTES),
        name="merge_mlp",
    )(x2, ya, yb_slab, gates, glu_w.astype(BF16), vec(glu_b), wa, wb, w_out.astype(BF16),
      vec(g_mlp), w_mlp_in.astype(BF16), w_mlp_out.astype(BF16), vec(g_final))


def kernel(x, norm_mix_g, w_in, conv_w, conv_b, dt_bias, a_log, d_ssd, ssd_norm_g, s5_a_re, s5_a_im, s5_log_dt, s5_b_re, s5_b_im, s5_c_re, s5_c_im, s5_d, s5_glu_w, s5_glu_b, w_branch, w_out, norm_mlp_g, w_mlp_in, w_mlp_out, norm_final_g):
    bsz, seqlen, _ = x.shape
    t = bsz * seqlen
    assert seqlen % SSD_ROW_TILE == 0 and t % ROW_TILE == 0 and t % IN_PROJ_ROW_TILE == 0
    assert seqlen % (S5_BLOCKS_PER_STEP * S5_BLOCK) == 0
    x2 = x.reshape(t, D_MODEL)

    zs, xbct, dt, dtt, u_slab, gates = _in_proj(x2, norm_mix_g, w_in)
    ya = _ssd(xbct, zs, dt, dtt, conv_w, conv_b, dt_bias, a_log, d_ssd, ssd_norm_g, bsz,
              seqlen)
    tables = _s5_tables(*_s5_prep(s5_a_re, s5_a_im, s5_log_dt, s5_b_re, s5_b_im,
                                  s5_c_re, s5_c_im), s5_d)
    yb_slab = _s5(u_slab, *tables, bsz)
    out = _merge_mlp(x2, ya, yb_slab, gates, s5_glu_w, s5_glu_b, w_branch, w_out, norm_mlp_g,
                     w_mlp_in, w_mlp_out, norm_final_g)
    return out.reshape(bsz, seqlen, D_MODEL)
```

```python
import jax
import jax.numpy as jnp
from jax import lax
from jax.experimental import pallas as pl
from jax.experimental.pallas import tpu as pltpu

F32 = jnp.float32
BF16 = jnp.bfloat16

D_MODEL = 1024
SSD_D_INNER = D_MODEL
SSD_HEADDIM = 64
SSD_N_HEADS = SSD_D_INNER // SSD_HEADDIM
SSD_N_GROUPS = 4
SSD_HPG = SSD_N_HEADS // SSD_N_GROUPS
SSD_D_STATE = 128
SSD_CONV = 4
SSD_CHUNK = 128
SSD_GROUP_W = SSD_D_INNER // SSD_N_GROUPS
SSD_BC_W = SSD_N_GROUPS * SSD_D_STATE
SSD_CONV_DIM = SSD_D_INNER + 2 * SSD_BC_W
S5_WIDTH = D_MODEL // 2
S5_GROUP = 16
S5_N_GROUPS = S5_WIDTH // S5_GROUP
S5_STATE = 64
S5_BLOCK = 16
S5_BLOCK_W = S5_BLOCK * S5_GROUP
D_FF = 4 * D_MODEL
N_BRANCHES = 2
OFF_Z = 0
OFF_XBC = OFF_Z + SSD_D_INNER
OFF_DT = OFF_XBC + SSD_CONV_DIM
OFF_U = OFF_DT + SSD_N_HEADS
OFF_G = OFF_U + S5_WIDTH
EPS = 1e-6

V7X_LANES = 128
V7X_SUBLANES = 8
V7X_VMEM_LIMIT_BYTES = 56 * 1024 * 1024

ROW_TILE = 512
IN_PROJ_ROW_TILE = 1024
SSD_ROW_TILE = 1024
S5_BLOCKS_PER_STEP = 128
CONV_ROWS = 512
S5_STATE_PITCH = 24


def _const_spec(shape):
    zeros = (0,) * len(shape)
    return pl.BlockSpec(shape, lambda *_: zeros, pipeline_mode=pl.Buffered(1))


def _rms(x):
    return x * lax.rsqrt(jnp.mean(x * x, axis=-1, keepdims=True) + EPS)


def _split3(x):
    hi = x.astype(BF16)
    r1 = x - hi.astype(F32)
    mid = r1.astype(BF16)
    lo = (r1 - mid.astype(F32)).astype(BF16)
    return hi, mid, lo


def _softplus(x):
    return jnp.maximum(x, 0.0) + jnp.log(1.0 + jnp.exp(-jnp.abs(x)))


GELU_C0 = 0.7978845608028654
GELU_C1 = GELU_C0 * 0.044715


def _gelu_tanh(x):
    h = 0.5 * x
    return h + h * jnp.tanh(x * (GELU_C0 + GELU_C1 * (x * x)))


def _silu(x):
    h = 0.5 * x
    return h + h * jnp.tanh(h)


def _s5_prep_kernel(are_ref, aim_ref, ldt_ref, bre_ref, bim_ref, cre_ref, cim_ref,
                    mt_ref, ere_ref, eim_ref, fre_ref, fim_ref, pwre_ref, pwim_ref):
    a_re = are_ref[...]
    a_im = aim_ref[...]
    dt = jnp.exp(ldt_ref[...])
    mag = jnp.exp(a_re * dt)
    ab_re = mag * jnp.cos(a_im * dt)
    ab_im = mag * jnp.sin(a_im * dt)
    den = a_re * a_re + a_im * a_im
    nr = ab_re - 1.0
    ni = ab_im
    coef_re = (nr * a_re + ni * a_im) / den
    coef_im = (ni * a_re - nr * a_im) / den
    b_re = bre_ref[...]
    b_im = bim_ref[...]
    bb_re = coef_re * b_re - coef_im * b_im
    bb_im = coef_re * b_im + coef_im * b_re
    c_re = cre_ref[...]
    c_im = cim_ref[...]

    p_re = jnp.ones_like(ab_re)
    p_im = jnp.zeros_like(ab_re)
    pb_re = [None] * S5_BLOCK
    pb_im = [None] * S5_BLOCK
    for tau in range(S5_BLOCK + 1):
        if tau < S5_BLOCK:
            s = S5_BLOCK - 1 - tau
            pb_re[s] = p_re * bb_re - p_im * bb_im
            pb_im[s] = p_re * bb_im + p_im * bb_re
            ere_ref[s] = pb_re[s]
            eim_ref[s] = pb_im[s]
        if tau >= 1:
            fre_ref[tau - 1] = c_re * p_re - c_im * p_im
            fim_ref[tau - 1] = -(c_re * p_im + c_im * p_re)
        if tau == S5_BLOCK:
            pwre_ref[...] = p_re
            pwim_ref[...] = p_im
        p_re, p_im = (p_re * ab_re - p_im * ab_im, p_re * ab_im + p_im * ab_re)

    dn = (((2,), (2,)), ((0,), (0,)))
    krev = (lax.dot_general(c_re, jnp.concatenate(pb_re, axis=1), dn,
                            precision=lax.Precision.HIGHEST, preferred_element_type=F32)
            - lax.dot_general(c_im, jnp.concatenate(pb_im, axis=1), dn,
                              precision=lax.Precision.HIGHEST, preferred_element_type=F32))
    lane = lax.broadcasted_iota(jnp.int32, krev.shape, 2)
    for t in range(S5_BLOCK):
        keep = S5_GROUP * (t + 1)
        moved = pltpu.roll(krev, keep % S5_BLOCK_W, 2)
        mt_ref[:, t * S5_GROUP:(t + 1) * S5_GROUP, :] = jnp.where(
            lane < keep, moved, 0.0).astype(BF16)


def _s5_prep(a_re, a_im, log_dt, b_re, b_im, c_re, c_im):
    g = S5_N_GROUPS
    bt_re = jnp.transpose(b_re, (0, 2, 1))
    bt_im = jnp.transpose(b_im, (0, 2, 1))
    tab = jax.ShapeDtypeStruct((S5_BLOCK, g, S5_GROUP, S5_STATE), F32)
    return pl.pallas_call(
        _s5_prep_kernel,
        out_shape=(jax.ShapeDtypeStruct((g, S5_BLOCK_W, S5_BLOCK_W), BF16), tab, tab, tab, tab,
                   jax.ShapeDtypeStruct((g, 1, S5_STATE), F32),
                   jax.ShapeDtypeStruct((g, 1, S5_STATE), F32)),
        name="s5_prep",
    )(a_re.reshape(g, 1, S5_STATE), a_im.reshape(g, 1, S5_STATE), log_dt.reshape(g, 1, 1),
      bt_re, bt_im, c_re, c_im)


def _in_proj_kernel(x_ref, g_ref, wz_ref, wxbct_ref, wdt_ref, wu_ref, wg_ref,
                    zs_ref, xbct_ref, dt_ref, dtt_ref, u_ref, gate_ref):
    h = (_rms(x_ref[...]) * g_ref[...]).astype(BF16)
    xbct_ref[...] = lax.dot_general(wxbct_ref[...], h, (((1,), (1,)), ((), ())),
                                    preferred_element_type=F32).astype(BF16)
    zs_ref[...] = _silu(jnp.dot(h, wz_ref[...], preferred_element_type=F32)).astype(BF16)
    u = jnp.dot(h, wu_ref[...], preferred_element_type=F32)
    for j in range(u_ref.shape[0]):
        u_ref[j] = u[:, j * V7X_LANES:(j + 1) * V7X_LANES]
    gate_ref[...] = jax.nn.sigmoid(
        jnp.dot(h, wg_ref[...], preferred_element_type=F32)).astype(BF16)
    dt = jnp.dot(h, wdt_ref[...], preferred_element_type=F32)
    dt_ref[...] = dt[:, :SSD_N_HEADS]
    for c in range(dtt_ref.shape[0]):
        dtt_ref[c] = dt[c * SSD_CHUNK:(c + 1) * SSD_CHUNK, :].T[:SSD_N_HEADS, :]


def _in_proj(x2, g_mix, w_in):
    t = x2.shape[0]
    tm = IN_PROJ_ROW_TILE
    wz = w_in[:, OFF_Z:OFF_XBC].astype(BF16)
    wxbct = w_in[:, OFF_XBC:OFF_DT].T.astype(BF16)
    wdt_pad = jnp.pad(w_in[:, OFF_DT:OFF_U],
                      ((0, 0), (0, V7X_LANES - SSD_N_HEADS))).astype(BF16)
    wu = w_in[:, OFF_U:OFF_G].astype(BF16)
    wg = w_in[:, OFF_G:].astype(BF16)
    n_slab = S5_WIDTH // V7X_LANES
    row = lambda w: pl.BlockSpec((tm, w), lambda i: (i, 0))
    return pl.pallas_call(
        _in_proj_kernel,
        grid=(t // tm,),
        in_specs=[row(D_MODEL), _const_spec((1, D_MODEL)), _const_spec(wz.shape),
                  _const_spec(wxbct.shape), _const_spec(wdt_pad.shape),
                  _const_spec(wu.shape), _const_spec(wg.shape)],
        out_specs=[row(SSD_D_INNER),
                   pl.BlockSpec((SSD_CONV_DIM, tm), lambda i: (0, i)),
                   row(SSD_N_HEADS),
                   pl.BlockSpec((tm // SSD_CHUNK, SSD_N_HEADS, SSD_CHUNK), lambda i: (i, 0, 0)),
                   pl.BlockSpec((n_slab, tm, V7X_LANES), lambda i: (0, i, 0)),
                   row(N_BRANCHES * D_MODEL)],
        out_shape=(jax.ShapeDtypeStruct((t, SSD_D_INNER), BF16),
                   jax.ShapeDtypeStruct((SSD_CONV_DIM, t), BF16),
                   jax.ShapeDtypeStruct((t, SSD_N_HEADS), F32),
                   jax.ShapeDtypeStruct((t // SSD_CHUNK, SSD_N_HEADS, SSD_CHUNK), F32),
                   jax.ShapeDtypeStruct((n_slab, t, V7X_LANES), F32),
                   jax.ShapeDtypeStruct((t, N_BRANCHES * D_MODEL), BF16)),
        compiler_params=pltpu.CompilerParams(dimension_semantics=("arbitrary",),
                                             vmem_limit_bytes=V7X_VMEM_LIMIT_BYTES),
        name="in_proj",
    )(x2, g_mix.reshape(1, D_MODEL), wz, wxbct, wdt_pad, wu, wg)


def _ssd_kernel(xbct_ref, zs_ref, dt_ref, dtt_ref, cwl_ref, cbl_ref, dtbc_ref, dtbr_ref,
                ac_ref, ar_ref, dcol_ref, ng_ref, y_ref, xprev, ht_s, st_s):
    tm = xbct_ref.shape[1]
    nch = tm // SSD_CHUNK
    q = SSD_CHUNK

    @pl.when(pl.program_id(1) == 0)
    def _():
        st_s[...] = jnp.zeros_like(st_s)
        xprev[...] = jnp.zeros_like(xprev)

    sh_r = lax.broadcasted_iota(jnp.int32, (2 * q, q), 0)
    sh_c = lax.broadcasted_iota(jnp.int32, (2 * q, q), 1)
    shift_all = jnp.concatenate(
        [(sh_r == sh_c + (q - (SSD_CONV - 1) + k)).astype(BF16) for k in range(SSD_CONV - 1)],
        axis=1)

    def conv_t(c, c0, n):
        if c == 0:
            win = jnp.concatenate([xprev[c0:c0 + n, :], xbct_ref[c0:c0 + n, 0:q]], axis=1)
        else:
            win = xbct_ref[c0:c0 + n, (c - 1) * q:(c + 1) * q]
        sh = jnp.dot(win, shift_all, preferred_element_type=F32)
        acc = cbl_ref[c0:c0 + n, :] + cwl_ref[SSD_CONV - 1, c0:c0 + n, :] * win[:, q:].astype(F32)
        for k in range(SSD_CONV - 1):
            acc = acc + cwl_ref[k, c0:c0 + n, :] * sh[:, k * q:(k + 1) * q]
        return acc + acc * jnp.tanh(acc)

    row_i = lax.broadcasted_iota(jnp.int32, (q, q), 0)
    col_i = lax.broadcasted_iota(jnp.int32, (q, q), 1)
    causal = row_i >= col_i
    tri = causal.astype(BF16)
    tri_t = (row_i <= col_i).astype(BF16)
    head_of_lane = lax.broadcasted_iota(jnp.int32, (q, SSD_GROUP_W), 1) // SSD_HEADDIM
    widen = (lax.broadcasted_iota(jnp.int32, (SSD_N_HEADS, SSD_D_INNER), 1) // SSD_HEADDIM
             == lax.broadcasted_iota(jnp.int32, (SSD_N_HEADS, SSD_D_INNER), 0)).astype(BF16)
    a_c = -jnp.exp(ac_ref[...])
    a_r = -jnp.exp(ar_ref[...])

    def chunk(c):
        r0 = c * q
        for c0 in range(0, SSD_CONV_DIM, CONV_ROWS):
            ht_s[c0:c0 + CONV_ROWS, :] = conv_t(c, c0, CONV_ROWS)
        dt_c = _softplus(dt_ref[pl.ds(r0, q), :] + dtbc_ref[...])
        dt_r = _softplus(dtt_ref[c] + dtbr_ref[...])
        la_c = dt_c * a_c
        la_r = dt_r * a_r
        cum_c = sum(jnp.dot(tri, p, preferred_element_type=F32) for p in _split3(la_c))
        cum_r = sum(jnp.dot(p, tri_t, preferred_element_type=F32) for p in _split3(la_r))
        cum_rl = cum_r - jnp.log(dt_r)
        to_end = dt_c * jnp.exp(cum_c[q - 1:q, :] - cum_c)
        te_x = jnp.dot(to_end.astype(BF16), widen, preferred_element_type=F32)
        ec_x = jnp.dot(jnp.exp(cum_c).astype(BF16), widen, preferred_element_type=F32)

        for g in range(SSD_N_GROUPS):
            b0 = SSD_D_INNER + g * SSD_D_STATE
            c0 = SSD_D_INNER + SSD_BC_W + g * SSD_D_STATE
            bgt = ht_s[b0:b0 + SSD_D_STATE, :].astype(BF16)
            cg = ht_s[c0:c0 + SSD_D_STATE, :].T.astype(BF16)
            cb = jnp.dot(cg, bgt, preferred_element_type=F32)
            ws = []
            for r in range(SSD_HPG):
                h = g * SSD_HPG + r
                cc = jnp.broadcast_to(cum_c[:, h:h + 1], (q, q))
                cr = jnp.broadcast_to(cum_rl[h:h + 1, :], (q, q))
                lm = jnp.exp(jnp.where(causal, cc - cr, -jnp.inf))
                ws.append((cb * lm).astype(BF16))
            wcat = jnp.concatenate(ws, axis=1)
            cols = slice(g * SSD_GROUP_W, (g + 1) * SSD_GROUP_W)
            xs_g = ht_s[g * SSD_GROUP_W:(g + 1) * SSD_GROUP_W, :].T
            xs_b = xs_g.astype(BF16)
            xdtd = (xs_g * te_x[:, cols]).astype(BF16)
            xbd = jnp.concatenate(
                [jnp.where(head_of_lane == r, xs_b, jnp.zeros_like(xs_b))
                 for r in range(SSD_HPG)], axis=0)
            y = jnp.dot(wcat, xbd, preferred_element_type=F32)
            prev = st_s[:, cols]
            y = y + jnp.dot(cg, prev.astype(BF16), preferred_element_type=F32) * ec_x[:, cols]
            st_s[:, cols] = prev * ec_x[q - 1:q, cols] + jnp.dot(
                bgt, xdtd, preferred_element_type=F32)
            y = y + dcol_ref[:, cols] * xs_g
            y = y * zs_ref[pl.ds(r0, q), cols].astype(F32)
            y_ref[pl.ds(r0, q), cols] = (_rms(y) * ng_ref[:, cols]).astype(BF16)

    for c in range(nch):
        chunk(c)
    xprev[...] = xbct_ref[:, tm - q:tm]


def _ssd(xbct, zs, dt, dtt, conv_w, conv_b, dt_bias, a_log, d_ssd, ssd_norm_g, bsz, seqlen):
    tm = SSD_ROW_TILE
    nt = seqlen // tm
    nch = tm // SSD_CHUNK
    h = SSD_N_HEADS
    row = lambda w: pl.BlockSpec((tm, w), lambda b, i: (b * nt + i, 0))
    cwl = jnp.broadcast_to((0.5 * conv_w)[:, :, None], (SSD_CONV, SSD_CONV_DIM, V7X_LANES))
    cbl = jnp.broadcast_to((0.5 * conv_b)[:, None], (SSD_CONV_DIM, V7X_LANES))
    return pl.pallas_call(
        _ssd_kernel,
        grid=(bsz, nt),
        in_specs=[pl.BlockSpec((SSD_CONV_DIM, tm), lambda b, i: (0, b * nt + i)),
                  row(SSD_D_INNER), row(h),
                  pl.BlockSpec((nch, h, SSD_CHUNK), lambda b, i: (b * nt + i, 0, 0)),
                  _const_spec(cwl.shape), _const_spec(cbl.shape),
                  _const_spec((1, h)), _const_spec((h, 1)), _const_spec((1, h)),
                  _const_spec((h, 1)), _const_spec((1, SSD_D_INNER)),
                  _const_spec((1, SSD_D_INNER))],
        out_specs=row(SSD_D_INNER),
        out_shape=jax.ShapeDtypeStruct((bsz * seqlen, SSD_D_INNER), BF16),
        scratch_shapes=[pltpu.VMEM((SSD_CONV_DIM, SSD_CHUNK), BF16),
                        pltpu.VMEM((SSD_CONV_DIM, SSD_CHUNK), F32),
                        pltpu.VMEM((SSD_D_STATE, SSD_D_INNER), F32)],
        compiler_params=pltpu.CompilerParams(dimension_semantics=("arbitrary", "arbitrary"),
                                             vmem_limit_bytes=V7X_VMEM_LIMIT_BYTES),
        name="ssd",
    )(xbct, zs, dt, dtt, cwl, cbl, dt_bias.reshape(1, h), dt_bias.reshape(h, 1),
      a_log.reshape(1, h), a_log.reshape(h, 1),
      jnp.repeat(d_ssd, SSD_HEADDIM).reshape(1, SSD_D_INNER), ssd_norm_g.reshape(1, -1))


def _s5_kernel(u_ref, mt_ref, et_ref, ft_ref, pre_ref, pim_ref, d_ref, y_ref,
               ut_s, yt_s, sre, sim, cre, cim):
    n_slab = u_ref.shape[0]
    nblk = u_ref.shape[1] // S5_BLOCK
    npair = S5_N_GROUPS // 2
    gps = V7X_LANES // S5_GROUP
    hw = V7X_LANES

    @pl.when(pl.program_id(1) == 0)
    def _():
        cre[...] = jnp.zeros_like(cre)
        cim[...] = jnp.zeros_like(cim)

    for s in range(S5_BLOCK):
        for j in range(n_slab):
            xt = u_ref[j, pl.ds(s, nblk, stride=S5_BLOCK), :].astype(BF16).T
            for gl in range(gps):
                ut_s[j * gps + gl, s * S5_GROUP:(s + 1) * S5_GROUP, :] = (
                    xt[gl * S5_GROUP:(gl + 1) * S5_GROUP, :])

    for pr in range(npair):
        ucat = jnp.concatenate([ut_s[2 * pr], ut_s[2 * pr + 1]], axis=0)
        sl = jnp.dot(et_ref[pr], ucat, preferred_element_type=F32).T
        sre[pl.ds(pr, nblk, stride=S5_STATE_PITCH), :] = sl[:, :hw]
        sim[pl.ds(pr, nblk, stride=S5_STATE_PITCH), :] = sl[:, hw:]

    a_re = pre_ref[...]
    a_im = pim_ref[...]

    def step(r, carry):
        c_re, c_im = carry
        rows = pl.ds(pl.multiple_of(r * S5_STATE_PITCH, V7X_SUBLANES), npair)
        l_re = sre[rows, :]
        l_im = sim[rows, :]
        sre[rows, :] = c_re
        sim[rows, :] = c_im
        return (a_re * c_re - a_im * c_im + l_re, a_re * c_im + a_im * c_re + l_im)

    c_re, c_im = lax.fori_loop(0, nblk, step, (cre[...], cim[...]))
    cre[...] = c_re
    cim[...] = c_im

    for pr in range(npair):
        sp = jnp.concatenate([sre[pl.ds(pr, nblk, stride=S5_STATE_PITCH), :],
                              sim[pl.ds(pr, nblk, stride=S5_STATE_PITCH), :]], axis=1).astype(BF16)
        cross = lax.dot_general(ft_ref[pr], sp, (((1,), (1,)), ((), ())),
                                preferred_element_type=F32)
        for j in range(2):
            gi = 2 * pr + j
            ug = ut_s[gi]
            y = (jnp.dot(mt_ref[gi], ug, preferred_element_type=F32)
                 + cross[j * S5_BLOCK_W:(j + 1) * S5_BLOCK_W, :]
                 + d_ref[gi] * ug.astype(F32))
            yt_s[gi] = _gelu_tanh(y).astype(BF16)

    for t in range(S5_BLOCK):
        for j in range(n_slab):
            zt = jnp.concatenate(
                [yt_s[j * gps + gl, t * S5_GROUP:(t + 1) * S5_GROUP, :] for gl in range(gps)],
                axis=0)
            y_ref[j, pl.ds(t, nblk, stride=S5_BLOCK), :] = zt.T.astype(F32)


def _s5(u_slab, mt, et, ft, pw_re, pw_im, d_col, bsz):
    n_slab, t, _ = u_slab.shape
    nblk = S5_BLOCKS_PER_STEP
    rows = nblk * S5_BLOCK
    parts = t // bsz // rows
    g, w = S5_N_GROUPS, S5_BLOCK_W
    io = pl.BlockSpec((n_slab, rows, V7X_LANES), lambda b, p: (0, b * parts + p, 0))
    state = pltpu.VMEM((nblk * S5_STATE_PITCH, V7X_LANES), F32)
    carry = pltpu.VMEM((g // 2, V7X_LANES), F32)
    return pl.pallas_call(
        _s5_kernel,
        grid=(bsz, parts),
        in_specs=[io, _const_spec(mt.shape), _const_spec(et.shape), _const_spec(ft.shape),
                  _const_spec(pw_re.shape), _const_spec(pw_im.shape), _const_spec(d_col.shape)],
        out_specs=io,
        out_shape=jax.ShapeDtypeStruct(u_slab.shape, F32),
        scratch_shapes=[pltpu.VMEM((g, w, nblk), BF16), pltpu.VMEM((g, w, nblk), BF16),
                        state, state, carry, carry],
        compiler_params=pltpu.CompilerParams(dimension_semantics=("arbitrary", "arbitrary"),
                                             vmem_limit_bytes=V7X_VMEM_LIMIT_BYTES),
        name="s5",
    )(u_slab, mt, et, ft, pw_re, pw_im, d_col)


def _s5_tables(mt, e_re, e_im, f_re, f_im, pw_re, pw_im, s5_d):
    g, q, w = S5_N_GROUPS, S5_BLOCK, S5_BLOCK_W
    eg_re = jnp.transpose(e_re, (1, 0, 2, 3)).reshape(g // 2, 2, w, S5_STATE)
    eg_im = jnp.transpose(e_im, (1, 0, 2, 3)).reshape(g // 2, 2, w, S5_STATE)
    zero = jnp.zeros_like(eg_re[:, 0])
    e_top = jnp.concatenate([eg_re[:, 0], zero, eg_im[:, 0], zero], axis=-1)
    e_bot = jnp.concatenate([zero, eg_re[:, 1], zero, eg_im[:, 1]], axis=-1)
    e = jnp.concatenate([e_top, e_bot], axis=1)
    fg_re = jnp.transpose(f_re, (1, 3, 0, 2)).reshape(g // 2, 2, S5_STATE, w)
    fg_im = jnp.transpose(f_im, (1, 3, 0, 2)).reshape(g // 2, 2, S5_STATE, w)
    zf = jnp.zeros_like(fg_re[:, 0])
    f = jnp.concatenate([
        jnp.concatenate([fg_re[:, 0], zf], axis=-1),
        jnp.concatenate([zf, fg_re[:, 1]], axis=-1),
        jnp.concatenate([fg_im[:, 0], zf], axis=-1),
        jnp.concatenate([zf, fg_im[:, 1]], axis=-1)], axis=1)
    tr = lambda a: jnp.swapaxes(a, 1, 2).astype(BF16)
    pw_re = pw_re.reshape(g // 2, 2 * S5_STATE)
    pw_im = pw_im.reshape(g // 2, 2 * S5_STATE)
    d_col = jnp.tile(s5_d.reshape(g, 1, S5_GROUP), (1, q, 1)).reshape(g, w, 1)
    return mt, tr(e), tr(f), pw_re, pw_im, d_col


def _merge_mlp_kernel(x_ref, ya_ref, yb_ref, gate_ref, gw_ref, gb_ref, wa_ref, wb_ref, wo_ref,
                      g2_ref, wi_ref, wo2_ref, g3_ref, o_ref):
    ybf = jnp.concatenate([yb_ref[j] for j in range(yb_ref.shape[0])], axis=1)
    yb = ybf.astype(BF16)
    glu = jnp.dot(yb, gw_ref[...], preferred_element_type=F32) + gb_ref[...]
    ybg = (ybf * jax.nn.sigmoid(glu)).astype(BF16)
    pa = jnp.dot(ya_ref[...], wa_ref[...], preferred_element_type=F32)
    pb = jnp.dot(ybg, wb_ref[...], preferred_element_type=F32)
    merged = (gate_ref[:, :D_MODEL].astype(F32) * pa
              + gate_ref[:, D_MODEL:].astype(F32) * pb).astype(BF16)
    x1 = x_ref[...] + jnp.dot(merged, wo_ref[...], preferred_element_type=F32)
    h2 = (_rms(x1) * g2_ref[...]).astype(BF16)
    acc = x1
    for k in range(D_FF // D_MODEL):
        cols = slice(k * D_MODEL, (k + 1) * D_MODEL)
        hid = jnp.dot(h2, wi_ref[:, cols], preferred_element_type=F32)
        hid = jnp.square(jnp.maximum(hid, 0.0)).astype(BF16)
        acc = acc + jnp.dot(hid, wo2_ref[cols, :], preferred_element_type=F32)
    o_ref[...] = _rms(acc) * g3_ref[...]


def _merge_mlp(x2, ya, yb_slab, gates, glu_w, glu_b, w_branch, w_out, g_mlp, w_mlp_in,
               w_mlp_out, g_final):
    t = x2.shape[0]
    tm = ROW_TILE
    wa = w_branch[:SSD_D_INNER].astype(BF16)
    wb = w_branch[SSD_D_INNER:].astype(BF16)
    row = lambda w: pl.BlockSpec((tm, w), lambda i: (i, 0))
    vec = lambda v: v.reshape(1, -1)
    return pl.pallas_call(
        _merge_mlp_kernel,
        grid=(t // tm,),
        in_specs=[row(D_MODEL), row(SSD_D_INNER),
                  pl.BlockSpec((yb_slab.shape[0], tm, V7X_LANES), lambda i: (0, i, 0)),
                  row(N_BRANCHES * D_MODEL),
                  _const_spec((S5_WIDTH, S5_WIDTH)), _const_spec((1, S5_WIDTH)),
                  _const_spec(wa.shape), _const_spec(wb.shape), _const_spec((D_MODEL, D_MODEL)),
                  _const_spec((1, D_MODEL)), _const_spec((D_MODEL, D_FF)),
                  _const_spec((D_FF, D_MODEL)), _const_spec((1, D_MODEL))],
        out_specs=row(D_MODEL),
        out_shape=jax.ShapeDtypeStruct((t, D_MODEL), F32),
        compiler_params=pltpu.CompilerParams(dimension_semantics=("arbitrary",),
                                             vmem_limit_bytes=V7X_VMEM_LIMIT_BYTES),
        name="merge_mlp",
    )(x2, ya, yb_slab, gates, glu_w.astype(BF16), vec(glu_b), wa, wb, w_out.astype(BF16),
      vec(g_mlp), w_mlp_in.astype(BF16), w_mlp_out.astype(BF16), vec(g_final))


def kernel(x, norm_mix_g, w_in, conv_w, conv_b, dt_bias, a_log, d_ssd, ssd_norm_g, s5_a_re, s5_a_im, s5_log_dt, s5_b_re, s5_b_im, s5_c_re, s5_c_im, s5_d, s5_glu_w, s5_glu_b, w_branch, w_out, norm_mlp_g, w_mlp_in, w_mlp_out, norm_final_g):
    bsz, seqlen, _ = x.shape
    t = bsz * seqlen
    assert seqlen % SSD_ROW_TILE == 0 and t % ROW_TILE == 0 and t % IN_PROJ_ROW_TILE == 0
    assert seqlen % (S5_BLOCKS_PER_STEP * S5_BLOCK) == 0
    x2 = x.reshape(t, D_MODEL)

    zs, xbct, dt, dtt, u_slab, gates = _in_proj(x2, norm_mix_g, w_in)
    ya = _ssd(xbct, zs, dt, dtt, conv_w, conv_b, dt_bias, a_log, d_ssd, ssd_norm_g, bsz,
              seqlen)
    tables = _s5_tables(*_s5_prep(s5_a_re, s5_a_im, s5_log_dt, s5_b_re, s5_b_im,
                                  s5_c_re, s5_c_im), s5_d)
    yb_slab = _s5(u_slab, *tables, bsz)
    out = _merge_mlp(x2, ya, yb_slab, gates, s5_glu_w, s5_glu_b, w_branch, w_out, norm_mlp_g,
                     w_mlp_in, w_mlp_out, norm_final_g)
    return out.reshape(bsz, seqlen, D_MODEL)
```

```python
import jax
import jax.numpy as jnp
from jax import lax
from jax.experimental import pallas as pl
from jax.experimental.pallas import tpu as pltpu

F32 = jnp.float32
BF16 = jnp.bfloat16

D_MODEL = 1024
SSD_D_INNER = D_MODEL
SSD_HEADDIM = 64
SSD_N_HEADS = SSD_D_INNER // SSD_HEADDIM
SSD_N_GROUPS = 4
SSD_HPG = SSD_N_HEADS // SSD_N_GROUPS
SSD_D_STATE = 128
SSD_CONV = 4
SSD_CHUNK = 128
SSD_GROUP_W = SSD_D_INNER // SSD_N_GROUPS
SSD_BC_W = SSD_N_GROUPS * SSD_D_STATE
SSD_CONV_DIM = SSD_D_INNER + 2 * SSD_BC_W
S5_WIDTH = D_MODEL // 2
S5_GROUP = 16
S5_N_GROUPS = S5_WIDTH // S5_GROUP
S5_STATE = 64
S5_BLOCK = 16
S5_BLOCK_W = S5_BLOCK * S5_GROUP
D_FF = 4 * D_MODEL
N_BRANCHES = 2
OFF_Z = 0
OFF_XBC = OFF_Z + SSD_D_INNER
OFF_DT = OFF_XBC + SSD_CONV_DIM
OFF_U = OFF_DT + SSD_N_HEADS
OFF_G = OFF_U + S5_WIDTH
EPS = 1e-6

V7X_LANES = 128
V7X_SUBLANES = 8
V7X_VMEM_LIMIT_BYTES = 56 * 1024 * 1024

ROW_TILE = 512
IN_PROJ_ROW_TILE = 1024
SSD_ROW_TILE = 1024
S5_BLOCKS_PER_STEP = 128
CONV_ROWS = 512
S5_STATE_PITCH = 24
S5_IL_STRIDE = 4


def _const_spec(shape):
    zeros = (0,) * len(shape)
    return pl.BlockSpec(shape, lambda *_: zeros, pipeline_mode=pl.Buffered(1))


def _rms(x):
    return x * lax.rsqrt(jnp.mean(x * x, axis=-1, keepdims=True) + EPS)


def _split3(x):
    hi = x.astype(BF16)
    r1 = x - hi.astype(F32)
    mid = r1.astype(BF16)
    lo = (r1 - mid.astype(F32)).astype(BF16)
    return hi, mid, lo


def _softplus(x):
    return jnp.maximum(x, 0.0) + jnp.log(1.0 + jnp.exp(-jnp.abs(x)))


GELU_C0 = 0.7978845608028654
GELU_C1 = GELU_C0 * 0.044715


def _gelu_tanh(x):
    h = 0.5 * x
    return h + h * jnp.tanh(x * (GELU_C0 + GELU_C1 * (x * x)))


def _silu(x):
    h = 0.5 * x
    return h + h * jnp.tanh(h)


def _s5_prep_kernel(are_ref, aim_ref, ldt_ref, bre_ref, bim_ref, cre_ref, cim_ref,
                    mt_ref, ere_ref, eim_ref, fre_ref, fim_ref, pwre_ref, pwim_ref):
    a_re = are_ref[...]
    a_im = aim_ref[...]
    dt = jnp.exp(ldt_ref[...])
    mag = jnp.exp(a_re * dt)
    ab_re = mag * jnp.cos(a_im * dt)
    ab_im = mag * jnp.sin(a_im * dt)
    den = a_re * a_re + a_im * a_im
    nr = ab_re - 1.0
    ni = ab_im
    coef_re = (nr * a_re + ni * a_im) / den
    coef_im = (ni * a_re - nr * a_im) / den
    b_re = bre_ref[...]
    b_im = bim_ref[...]
    bb_re = coef_re * b_re - coef_im * b_im
    bb_im = coef_re * b_im + coef_im * b_re
    c_re = cre_ref[...]
    c_im = cim_ref[...]

    p_re = jnp.ones_like(ab_re)
    p_im = jnp.zeros_like(ab_re)
    pb_re = [None] * S5_BLOCK
    pb_im = [None] * S5_BLOCK
    for tau in range(S5_BLOCK + 1):
        if tau < S5_BLOCK:
            s = S5_BLOCK - 1 - tau
            pb_re[s] = p_re * bb_re - p_im * bb_im
            pb_im[s] = p_re * bb_im + p_im * bb_re
            ere_ref[s] = pb_re[s]
            eim_ref[s] = pb_im[s]
        if tau >= 1:
            fre_ref[tau - 1] = c_re * p_re - c_im * p_im
            fim_ref[tau - 1] = -(c_re * p_im + c_im * p_re)
        if tau == S5_BLOCK:
            pwre_ref[...] = p_re
            pwim_ref[...] = p_im
        p_re, p_im = (p_re * ab_re - p_im * ab_im, p_re * ab_im + p_im * ab_re)

    dn = (((2,), (2,)), ((0,), (0,)))
    krev = (lax.dot_general(c_re, jnp.concatenate(pb_re, axis=1), dn,
                            precision=lax.Precision.HIGHEST, preferred_element_type=F32)
            - lax.dot_general(c_im, jnp.concatenate(pb_im, axis=1), dn,
                              precision=lax.Precision.HIGHEST, preferred_element_type=F32))
    lane = lax.broadcasted_iota(jnp.int32, krev.shape, 2)
    for t in range(S5_BLOCK):
        keep = S5_GROUP * (t + 1)
        moved = pltpu.roll(krev, keep % S5_BLOCK_W, 2)
        mt_ref[:, t * S5_GROUP:(t + 1) * S5_GROUP, :] = jnp.where(
            lane < keep, moved, 0.0).astype(BF16)


def _s5_prep(a_re, a_im, log_dt, b_re, b_im, c_re, c_im):
    g = S5_N_GROUPS
    bt_re = jnp.transpose(b_re, (0, 2, 1))
    bt_im = jnp.transpose(b_im, (0, 2, 1))
    tab = jax.ShapeDtypeStruct((S5_BLOCK, g, S5_GROUP, S5_STATE), F32)
    return pl.pallas_call(
        _s5_prep_kernel,
        out_shape=(jax.ShapeDtypeStruct((g, S5_BLOCK_W, S5_BLOCK_W), BF16), tab, tab, tab, tab,
                   jax.ShapeDtypeStruct((g, 1, S5_STATE), F32),
                   jax.ShapeDtypeStruct((g, 1, S5_STATE), F32)),
        name="s5_prep",
    )(a_re.reshape(g, 1, S5_STATE), a_im.reshape(g, 1, S5_STATE), log_dt.reshape(g, 1, 1),
      bt_re, bt_im, c_re, c_im)


def _in_proj_kernel(x_ref, g_ref, wz_ref, wxbct_ref, wdt_ref, wu_ref, wg_ref,
                    zs_ref, xbct_ref, dt_ref, dtt_ref, u_ref, gate_ref):
    h = (_rms(x_ref[...]) * g_ref[...]).astype(BF16)
    xbct_ref[...] = lax.dot_general(wxbct_ref[...], h, (((1,), (1,)), ((), ())),
                                    preferred_element_type=F32).astype(BF16)
    zs_ref[...] = _silu(jnp.dot(h, wz_ref[...], preferred_element_type=F32)).astype(BF16)
    u = jnp.dot(h, wu_ref[...], preferred_element_type=F32)
    for j in range(u_ref.shape[0]):
        u_ref[j] = u[:, j * V7X_LANES:(j + 1) * V7X_LANES]
    gate_ref[...] = jax.nn.sigmoid(
        jnp.dot(h, wg_ref[...], preferred_element_type=F32)).astype(BF16)
    dt = jnp.dot(h, wdt_ref[...], preferred_element_type=F32)
    dt_ref[...] = dt[:, :SSD_N_HEADS]
    for c in range(dtt_ref.shape[0]):
        dtt_ref[c] = dt[c * SSD_CHUNK:(c + 1) * SSD_CHUNK, :].T[:SSD_N_HEADS, :]


def _in_proj(x2, g_mix, w_in):
    t = x2.shape[0]
    tm = IN_PROJ_ROW_TILE
    wz = w_in[:, OFF_Z:OFF_XBC].astype(BF16)
    wxbct = w_in[:, OFF_XBC:OFF_DT].T.astype(BF16)
    wdt_pad = jnp.pad(w_in[:, OFF_DT:OFF_U],
                      ((0, 0), (0, V7X_LANES - SSD_N_HEADS))).astype(BF16)
    wu = w_in[:, OFF_U:OFF_G].astype(BF16)
    wg = w_in[:, OFF_G:].astype(BF16)
    n_slab = S5_WIDTH // V7X_LANES
    row = lambda w: pl.BlockSpec((tm, w), lambda i: (i, 0))
    return pl.pallas_call(
        _in_proj_kernel,
        grid=(t // tm,),
        in_specs=[row(D_MODEL), _const_spec((1, D_MODEL)), _const_spec(wz.shape),
                  _const_spec(wxbct.shape), _const_spec(wdt_pad.shape),
                  _const_spec(wu.shape), _const_spec(wg.shape)],
        out_specs=[row(SSD_D_INNER),
                   pl.BlockSpec((SSD_CONV_DIM, tm), lambda i: (0, i)),
                   row(SSD_N_HEADS),
                   pl.BlockSpec((tm // SSD_CHUNK, SSD_N_HEADS, SSD_CHUNK), lambda i: (i, 0, 0)),
                   pl.BlockSpec((n_slab, tm, V7X_LANES), lambda i: (0, i, 0)),
                   row(N_BRANCHES * D_MODEL)],
        out_shape=(jax.ShapeDtypeStruct((t, SSD_D_INNER), BF16),
                   jax.ShapeDtypeStruct((SSD_CONV_DIM, t), BF16),
                   jax.ShapeDtypeStruct((t, SSD_N_HEADS), F32),
                   jax.ShapeDtypeStruct((t // SSD_CHUNK, SSD_N_HEADS, SSD_CHUNK), F32),
                   jax.ShapeDtypeStruct((n_slab, t, V7X_LANES), F32),
                   jax.ShapeDtypeStruct((t, N_BRANCHES * D_MODEL), BF16)),
        compiler_params=pltpu.CompilerParams(dimension_semantics=("arbitrary",),
                                             vmem_limit_bytes=V7X_VMEM_LIMIT_BYTES),
        name="in_proj",
    )(x2, g_mix.reshape(1, D_MODEL), wz, wxbct, wdt_pad, wu, wg)


def _ssd_kernel(xbct_ref, zs_ref, dt_ref, dtt_ref, cwl_ref, cbl_ref, dtbc_ref, dtbr_ref,
                ac_ref, ar_ref, dcol_ref, ng_ref, y_ref, xprev, ht_s, st_s):
    tm = xbct_ref.shape[1]
    nch = tm // SSD_CHUNK
    q = SSD_CHUNK

    @pl.when(pl.program_id(1) == 0)
    def _():
        st_s[...] = jnp.zeros_like(st_s)
        xprev[...] = jnp.zeros_like(xprev)

    sh_r = lax.broadcasted_iota(jnp.int32, (2 * q, q), 0)
    sh_c = lax.broadcasted_iota(jnp.int32, (2 * q, q), 1)
    shift_all = jnp.concatenate(
        [(sh_r == sh_c + (q - (SSD_CONV - 1) + k)).astype(BF16) for k in range(SSD_CONV - 1)],
        axis=1)

    def conv_t(c, c0, n):
        if c == 0:
            win = jnp.concatenate([xprev[c0:c0 + n, :], xbct_ref[c0:c0 + n, 0:q]], axis=1)
        else:
            win = xbct_ref[c0:c0 + n, (c - 1) * q:(c + 1) * q]
        sh = jnp.dot(win, shift_all, preferred_element_type=F32)
        acc = cbl_ref[c0:c0 + n, :] + cwl_ref[SSD_CONV - 1, c0:c0 + n, :] * win[:, q:].astype(F32)
        for k in range(SSD_CONV - 1):
            acc = acc + cwl_ref[k, c0:c0 + n, :] * sh[:, k * q:(k + 1) * q]
        return acc + acc * jnp.tanh(acc)

    row_i = lax.broadcasted_iota(jnp.int32, (q, q), 0)
    col_i = lax.broadcasted_iota(jnp.int32, (q, q), 1)
    causal = row_i >= col_i
    tri = causal.astype(BF16)
    tri_t = (row_i <= col_i).astype(BF16)
    head_of_lane = lax.broadcasted_iota(jnp.int32, (q, SSD_GROUP_W), 1) // SSD_HEADDIM
    widen = (lax.broadcasted_iota(jnp.int32, (SSD_N_HEADS, SSD_D_INNER), 1) // SSD_HEADDIM
             == lax.broadcasted_iota(jnp.int32, (SSD_N_HEADS, SSD_D_INNER), 0)).astype(BF16)
    a_c = -jnp.exp(ac_ref[...])
    a_r = -jnp.exp(ar_ref[...])

    def chunk(c):
        r0 = c * q
        for c0 in range(0, SSD_CONV_DIM, CONV_ROWS):
            ht_s[c0:c0 + CONV_ROWS, :] = conv_t(c, c0, CONV_ROWS)
        dt_c = _softplus(dt_ref[pl.ds(r0, q), :] + dtbc_ref[...])
        dt_r = _softplus(dtt_ref[c] + dtbr_ref[...])
        la_c = dt_c * a_c
        la_r = dt_r * a_r
        cum_c = sum(jnp.dot(tri, p, preferred_element_type=F32) for p in _split3(la_c))
        cum_r = sum(jnp.dot(p, tri_t, preferred_element_type=F32) for p in _split3(la_r))
        cum_rl = cum_r - jnp.log(dt_r)
        to_end = dt_c * jnp.exp(cum_c[q - 1:q, :] - cum_c)
        te_x = jnp.dot(to_end.astype(BF16), widen, preferred_element_type=F32)
        ec_x = jnp.dot(jnp.exp(cum_c).astype(BF16), widen, preferred_element_type=F32)

        for g in range(SSD_N_GROUPS):
            b0 = SSD_D_INNER + g * SSD_D_STATE
            c0 = SSD_D_INNER + SSD_BC_W + g * SSD_D_STATE
            bgt = ht_s[b0:b0 + SSD_D_STATE, :].astype(BF16)
            cg = ht_s[c0:c0 + SSD_D_STATE, :].T.astype(BF16)
            cb = jnp.dot(cg, bgt, preferred_element_type=F32)
            ws = []
            for r in range(SSD_HPG):
                h = g * SSD_HPG + r
                cc = jnp.broadcast_to(cum_c[:, h:h + 1], (q, q))
                cr = jnp.broadcast_to(cum_rl[h:h + 1, :], (q, q))
                lm = jnp.exp(jnp.where(causal, cc - cr, -jnp.inf))
                ws.append((cb * lm).astype(BF16))
            wcat = jnp.concatenate(ws, axis=1)
            cols = slice(g * SSD_GROUP_W, (g + 1) * SSD_GROUP_W)
            xs_g = ht_s[g * SSD_GROUP_W:(g + 1) * SSD_GROUP_W, :].T
            xs_b = xs_g.astype(BF16)
            xdtd = (xs_g * te_x[:, cols]).astype(BF16)
            xbd = jnp.concatenate(
                [jnp.where(head_of_lane == r, xs_b, jnp.zeros_like(xs_b))
                 for r in range(SSD_HPG)], axis=0)
            y = jnp.dot(wcat, xbd, preferred_element_type=F32)
            prev = st_s[:, cols]
            y = y + jnp.dot(cg, prev.astype(BF16), preferred_element_type=F32) * ec_x[:, cols]
            st_s[:, cols] = prev * ec_x[q - 1:q, cols] + jnp.dot(
                bgt, xdtd, preferred_element_type=F32)
            y = y + dcol_ref[:, cols] * xs_g
            y = y * zs_ref[pl.ds(r0, q), cols].astype(F32)
            y_ref[pl.ds(r0, q), cols] = (_rms(y) * ng_ref[:, cols]).astype(BF16)

    for c in range(nch):
        chunk(c)
    xprev[...] = xbct_ref[:, tm - q:tm]


def _ssd(xbct, zs, dt, dtt, conv_w, conv_b, dt_bias, a_log, d_ssd, ssd_norm_g, bsz, seqlen):
    tm = SSD_ROW_TILE
    nt = seqlen // tm
    nch = tm // SSD_CHUNK
    h = SSD_N_HEADS
    row = lambda w: pl.BlockSpec((tm, w), lambda b, i: (b * nt + i, 0))
    cwl = jnp.broadcast_to((0.5 * conv_w)[:, :, None], (SSD_CONV, SSD_CONV_DIM, V7X_LANES))
    cbl = jnp.broadcast_to((0.5 * conv_b)[:, None], (SSD_CONV_DIM, V7X_LANES))
    return pl.pallas_call(
        _ssd_kernel,
        grid=(bsz, nt),
        in_specs=[pl.BlockSpec((SSD_CONV_DIM, tm), lambda b, i: (0, b * nt + i)),
                  row(SSD_D_INNER), row(h),
                  pl.BlockSpec((nch, h, SSD_CHUNK), lambda b, i: (b * nt + i, 0, 0)),
                  _const_spec(cwl.shape), _const_spec(cbl.shape),
                  _const_spec((1, h)), _const_spec((h, 1)), _const_spec((1, h)),
                  _const_spec((h, 1)), _const_spec((1, SSD_D_INNER)),
                  _const_spec((1, SSD_D_INNER))],
        out_specs=row(SSD_D_INNER),
        out_shape=jax.ShapeDtypeStruct((bsz * seqlen, SSD_D_INNER), BF16),
        scratch_shapes=[pltpu.VMEM((SSD_CONV_DIM, SSD_CHUNK), BF16),
                        pltpu.VMEM((SSD_CONV_DIM, SSD_CHUNK), F32),
                        pltpu.VMEM((SSD_D_STATE, SSD_D_INNER), F32)],
        compiler_params=pltpu.CompilerParams(dimension_semantics=("arbitrary", "arbitrary"),
                                             vmem_limit_bytes=V7X_VMEM_LIMIT_BYTES),
        name="ssd",
    )(xbct, zs, dt, dtt, cwl, cbl, dt_bias.reshape(1, h), dt_bias.reshape(h, 1),
      a_log.reshape(1, h), a_log.reshape(h, 1),
      jnp.repeat(d_ssd, SSD_HEADDIM).reshape(1, SSD_D_INNER), ssd_norm_g.reshape(1, -1))


def _s5_kernel(u_ref, mt_ref, et_ref, ft_ref, pre_ref, pim_ref, d_ref, y_ref,
               ut_s, yt_s, il_s, sre, sim, cre, cim):
    n_slab = u_ref.shape[0]
    nblk = u_ref.shape[1] // S5_BLOCK
    npair = S5_N_GROUPS // 2
    gps = V7X_LANES // S5_GROUP
    hw = V7X_LANES

    @pl.when(pl.program_id(1) == 0)
    def _():
        cre[...] = jnp.zeros_like(cre)
        cim[...] = jnp.zeros_like(cim)

    sub = S5_IL_STRIDE
    for j in range(n_slab):
        for c in range(sub):
            il_s[j, c] = u_ref[j, pl.ds(c, nblk * sub, stride=sub), :]
    for s in range(S5_BLOCK):
        for j in range(n_slab):
            xt = il_s[j, s % sub, pl.ds(s // sub, nblk, stride=sub), :].astype(BF16).T
            for gl in range(gps):
                ut_s[j * gps + gl, s * S5_GROUP:(s + 1) * S5_GROUP, :] = (
                    xt[gl * S5_GROUP:(gl + 1) * S5_GROUP, :])

    for pr in range(npair):
        ucat = jnp.concatenate([ut_s[2 * pr], ut_s[2 * pr + 1]], axis=0)
        sl = jnp.dot(et_ref[pr], ucat, preferred_element_type=F32).T
        sre[pl.ds(pr, nblk, stride=S5_STATE_PITCH), :] = sl[:, :hw]
        sim[pl.ds(pr, nblk, stride=S5_STATE_PITCH), :] = sl[:, hw:]

    a_re = pre_ref[...]
    a_im = pim_ref[...]

    def step(r, carry):
        c_re, c_im = carry
        rows = pl.ds(pl.multiple_of(r * S5_STATE_PITCH, V7X_SUBLANES), npair)
        l_re = sre[rows, :]
        l_im = sim[rows, :]
        sre[rows, :] = c_re
        sim[rows, :] = c_im
        return (a_re * c_re - a_im * c_im + l_re, a_re * c_im + a_im * c_re + l_im)

    c_re, c_im = lax.fori_loop(0, nblk, step, (cre[...], cim[...]))
    cre[...] = c_re
    cim[...] = c_im

    for pr in range(npair):
        sp = jnp.concatenate([sre[pl.ds(pr, nblk, stride=S5_STATE_PITCH), :],
                              sim[pl.ds(pr, nblk, stride=S5_STATE_PITCH), :]], axis=1).astype(BF16)
        cross = lax.dot_general(ft_ref[pr], sp, (((1,), (1,)), ((), ())),
                                preferred_element_type=F32)
        for j in range(2):
            gi = 2 * pr + j
            ug = ut_s[gi]
            y = (jnp.dot(mt_ref[gi], ug, preferred_element_type=F32)
                 + cross[j * S5_BLOCK_W:(j + 1) * S5_BLOCK_W, :]
                 + d_ref[gi] * ug.astype(F32))
            yt_s[gi] = _gelu_tanh(y).astype(BF16)

    for t in range(S5_BLOCK):
        for j in range(n_slab):
            zt = jnp.concatenate(
                [yt_s[j * gps + gl, t * S5_GROUP:(t + 1) * S5_GROUP, :] for gl in range(gps)],
                axis=0)
            il_s[j, t % sub, pl.ds(t // sub, nblk, stride=sub), :] = zt.T.astype(F32)
    for j in range(n_slab):
        for c in range(sub):
            y_ref[j, pl.ds(c, nblk * sub, stride=sub), :] = il_s[j, c]


def _s5(u_slab, mt, et, ft, pw_re, pw_im, d_col, bsz):
    n_slab, t, _ = u_slab.shape
    nblk = S5_BLOCKS_PER_STEP
    rows = nblk * S5_BLOCK
    parts = t // bsz // rows
    g, w = S5_N_GROUPS, S5_BLOCK_W
    io = pl.BlockSpec((n_slab, rows, V7X_LANES), lambda b, p: (0, b * parts + p, 0))
    state = pltpu.VMEM((nblk * S5_STATE_PITCH, V7X_LANES), F32)
    carry = pltpu.VMEM((g // 2, V7X_LANES), F32)
    return pl.pallas_call(
        _s5_kernel,
        grid=(bsz, parts),
        in_specs=[io, _const_spec(mt.shape), _const_spec(et.shape), _const_spec(ft.shape),
                  _const_spec(pw_re.shape), _const_spec(pw_im.shape), _const_spec(d_col.shape)],
        out_specs=io,
        out_shape=jax.ShapeDtypeStruct(u_slab.shape, F32),
        scratch_shapes=[pltpu.VMEM((g, w, nblk), BF16), pltpu.VMEM((g, w, nblk), BF16),
                        pltpu.VMEM((n_slab, S5_IL_STRIDE, rows // S5_IL_STRIDE, V7X_LANES), F32),
                        state, state, carry, carry],
        compiler_params=pltpu.CompilerParams(dimension_semantics=("arbitrary", "arbitrary"),
                                             vmem_limit_bytes=V7X_VMEM_LIMIT_BYTES),
        name="s5",
    )(u_slab, mt, et, ft, pw_re, pw_im, d_col)


def _s5_tables(mt, e_re, e_im, f_re, f_im, pw_re, pw_im, s5_d):
    g, q, w = S5_N_GROUPS, S5_BLOCK, S5_BLOCK_W
    eg_re = jnp.transpose(e_re, (1, 0, 2, 3)).reshape(g // 2, 2, w, S5_STATE)
    eg_im = jnp.transpose(e_im, (1, 0, 2, 3)).reshape(g // 2, 2, w, S5_STATE)
    zero = jnp.zeros_like(eg_re[:, 0])
    e_top = jnp.concatenate([eg_re[:, 0], zero, eg_im[:, 0], zero], axis=-1)
    e_bot = jnp.concatenate([zero, eg_re[:, 1], zero, eg_im[:, 1]], axis=-1)
    e = jnp.concatenate([e_top, e_bot], axis=1)
    fg_re = jnp.transpose(f_re, (1, 3, 0, 2)).reshape(g // 2, 2, S5_STATE, w)
    fg_im = jnp.transpose(f_im, (1, 3, 0, 2)).reshape(g // 2, 2, S5_STATE, w)
    zf = jnp.zeros_like(fg_re[:, 0])
    f = jnp.concatenate([
        jnp.concatenate([fg_re[:, 0], zf], axis=-1),
        jnp.concatenate([zf, fg_re[:, 1]], axis=-1),
        jnp.concatenate([fg_im[:, 0], zf], axis=-1),
        jnp.concatenate([zf, fg_im[:, 1]], axis=-1)], axis=1)
    tr = lambda a: jnp.swapaxes(a, 1, 2).astype(BF16)
    pw_re = pw_re.reshape(g // 2, 2 * S5_STATE)
    pw_im = pw_im.reshape(g // 2, 2 * S5_STATE)
    d_col = jnp.tile(s5_d.reshape(g, 1, S5_GROUP), (1, q, 1)).reshape(g, w, 1)
    return mt, tr(e), tr(f), pw_re, pw_im, d_col


def _merge_mlp_kernel(x_ref, ya_ref, yb_ref, gate_ref, gw_ref, gb_ref, wa_ref, wb_ref, wo_ref,
                      g2_ref, wi_ref, wo2_ref, g3_ref, o_ref):
    ybf = jnp.concatenate([yb_ref[j] for j in range(yb_ref.shape[0])], axis=1)
    yb = ybf.astype(BF16)
    glu = jnp.dot(yb, gw_ref[...], preferred_element_type=F32) + gb_ref[...]
    ybg = (ybf * jax.nn.sigmoid(glu)).astype(BF16)
    pa = jnp.dot(ya_ref[...], wa_ref[...], preferred_element_type=F32)
    pb = jnp.dot(ybg, wb_ref[...], preferred_element_type=F32)
    merged = (gate_ref[:, :D_MODEL].astype(F32) * pa
              + gate_ref[:, D_MODEL:].astype(F32) * pb).astype(BF16)
    x1 = x_ref[...] + jnp.dot(merged, wo_ref[...], preferred_element_type=F32)
    h2 = (_rms(x1) * g2_ref[...]).astype(BF16)
    acc = x1
    for k in range(D_FF // D_MODEL):
        cols = slice(k * D_MODEL, (k + 1) * D_MODEL)
        hid = jnp.dot(h2, wi_ref[:, cols], preferred_element_type=F32)
        hid = jnp.square(jnp.maximum(hid, 0.0)).astype(BF16)
        acc = acc + jnp.dot(hid, wo2_ref[cols, :], preferred_element_type=F32)
    o_ref[...] = _rms(acc) * g3_ref[...]


def _merge_mlp(x2, ya, yb_slab, gates, glu_w, glu_b, w_branch, w_out, g_mlp, w_mlp_in,
               w_mlp_out, g_final):
    t = x2.shape[0]
    tm = ROW_TILE
    wa = w_branch[:SSD_D_INNER].astype(BF16)
    wb = w_branch[SSD_D_INNER:].astype(BF16)
    row = lambda w: pl.BlockSpec((tm, w), lambda i: (i, 0))
    vec = lambda v: v.reshape(1, -1)
    return pl.pallas_call(
        _merge_mlp_kernel,
        grid=(t // tm,),
        in_specs=[row(D_MODEL), row(SSD_D_INNER),
                  pl.BlockSpec((yb_slab.shape[0], tm, V7X_LANES), lambda i: (0, i, 0)),
                  row(N_BRANCHES * D_MODEL),
                  _const_spec((S5_WIDTH, S5_WIDTH)), _const_spec((1, S5_WIDTH)),
                  _const_spec(wa.shape), _const_spec(wb.shape), _const_spec((D_MODEL, D_MODEL)),
                  _const_spec((1, D_MODEL)), _const_spec((D_MODEL, D_FF)),
                  _const_spec((D_FF, D_MODEL)), _const_spec((1, D_MODEL))],
        out_specs=row(D_MODEL),
        out_shape=jax.ShapeDtypeStruct((t, D_MODEL), F32),
        compiler_params=pltpu.CompilerParams(dimension_semantics=("arbitrary",),
                                             vmem_limit_bytes=V7X_VMEM_LIMIT_BYTES),
        name="merge_mlp",
    )(x2, ya, yb_slab, gates, glu_w.astype(BF16), vec(glu_b), wa, wb, w_out.astype(BF16),
      vec(g_mlp), w_mlp_in.astype(BF16), w_mlp_out.astype(BF16), vec(g_final))


def kernel(x, norm_mix_g, w_in, conv_w, conv_b, dt_bias, a_log, d_ssd, ssd_norm_g, s5_a_re, s5_a_im, s5_log_dt, s5_b_re, s5_b_im, s5_c_re, s5_c_im, s5_d, s5_glu_w, s5_glu_b, w_branch, w_out, norm_mlp_g, w_mlp_in, w_mlp_out, norm_final_g):
    bsz, seqlen, _ = x.shape
    t = bsz * seqlen
    assert seqlen % SSD_ROW_TILE == 0 and t % ROW_TILE == 0 and t % IN_PROJ_ROW_TILE == 0
    assert seqlen % (S5_BLOCKS_PER_STEP * S5_BLOCK) == 0
    x2 = x.reshape(t, D_MODEL)

    zs, xbct, dt, dtt, u_slab, gates = _in_proj(x2, norm_mix_g, w_in)
    ya = _ssd(xbct, zs, dt, dtt, conv_w, conv_b, dt_bias, a_log, d_ssd, ssd_norm_g, bsz,
              seqlen)
    tables = _s5_tables(*_s5_prep(s5_a_re, s5_a_im, s5_log_dt, s5_b_re, s5_b_im,
                                  s5_c_re, s5_c_im), s5_d)
    yb_slab = _s5(u_slab, *tables, bsz)
    out = _merge_mlp(x2, ya, yb_slab, gates, s5_glu_w, s5_glu_b, w_branch, w_out, norm_mlp_g,
                     w_mlp_in, w_mlp_out, norm_final_g)
    return out.reshape(bsz, seqlen, D_MODEL)
```

```python
import jax
import jax.numpy as jnp
from jax import lax
from jax.experimental import pallas as pl
from jax.experimental.pallas import tpu as pltpu

F32 = jnp.float32
BF16 = jnp.bfloat16

D_MODEL = 1024
SSD_D_INNER = D_MODEL
SSD_HEADDIM = 64
SSD_N_HEADS = SSD_D_INNER // SSD_HEADDIM
SSD_N_GROUPS = 4
SSD_HPG = SSD_N_HEADS // SSD_N_GROUPS
SSD_D_STATE = 128
SSD_CONV = 4
SSD_CHUNK = 128
SSD_GROUP_W = SSD_D_INNER // SSD_N_GROUPS
SSD_BC_W = SSD_N_GROUPS * SSD_D_STATE
SSD_CONV_DIM = SSD_D_INNER + 2 * SSD_BC_W
S5_WIDTH = D_MODEL // 2
S5_GROUP = 16
S5_N_GROUPS = S5_WIDTH // S5_GROUP
S5_STATE = 64
S5_BLOCK = 16
S5_BLOCK_W = S5_BLOCK * S5_GROUP
D_FF = 4 * D_MODEL
N_BRANCHES = 2
OFF_Z = 0
OFF_XBC = OFF_Z + SSD_D_INNER
OFF_DT = OFF_XBC + SSD_CONV_DIM
OFF_U = OFF_DT + SSD_N_HEADS
OFF_G = OFF_U + S5_WIDTH
EPS = 1e-6

V7X_LANES = 128
V7X_SUBLANES = 8
V7X_VMEM_LIMIT_BYTES = 56 * 1024 * 1024

ROW_TILE = 512
IN_PROJ_ROW_TILE = 1024
SSD_ROW_TILE = 1024
S5_BLOCKS_PER_STEP = 128
CONV_ROWS = 512
S5_STATE_PITCH = 24
S5_IL_STRIDE = 4


def _const_spec(shape):
    zeros = (0,) * len(shape)
    return pl.BlockSpec(shape, lambda *_: zeros, pipeline_mode=pl.Buffered(1))


def _rms(x):
    return x * lax.rsqrt(jnp.mean(x * x, axis=-1, keepdims=True) + EPS)


def _split3(x):
    hi = x.astype(BF16)
    r1 = x - hi.astype(F32)
    mid = r1.astype(BF16)
    lo = (r1 - mid.astype(F32)).astype(BF16)
    return hi, mid, lo


def _softplus(x):
    return jnp.maximum(x, 0.0) + jnp.log(1.0 + jnp.exp(-jnp.abs(x)))


GELU_C0 = 0.7978845608028654
GELU_C1 = GELU_C0 * 0.044715


def _gelu_tanh(x):
    h = 0.5 * x
    return h + h * jnp.tanh(x * (GELU_C0 + GELU_C1 * (x * x)))


def _silu(x):
    h = 0.5 * x
    return h + h * jnp.tanh(h)


def _s5_prep_kernel(are_ref, aim_ref, ldt_ref, bre_ref, bim_ref, cre_ref, cim_ref,
                    mt_ref, ere_ref, eim_ref, fre_ref, fim_ref, pwre_ref, pwim_ref):
    a_re = are_ref[...]
    a_im = aim_ref[...]
    dt = jnp.exp(ldt_ref[...])
    mag = jnp.exp(a_re * dt)
    ab_re = mag * jnp.cos(a_im * dt)
    ab_im = mag * jnp.sin(a_im * dt)
    den = a_re * a_re + a_im * a_im
    nr = ab_re - 1.0
    ni = ab_im
    coef_re = (nr * a_re + ni * a_im) / den
    coef_im = (ni * a_re - nr * a_im) / den
    b_re = bre_ref[...]
    b_im = bim_ref[...]
    bb_re = coef_re * b_re - coef_im * b_im
    bb_im = coef_re * b_im + coef_im * b_re
    c_re = cre_ref[...]
    c_im = cim_ref[...]

    p_re = jnp.ones_like(ab_re)
    p_im = jnp.zeros_like(ab_re)
    pb_re = [None] * S5_BLOCK
    pb_im = [None] * S5_BLOCK
    for tau in range(S5_BLOCK + 1):
        if tau < S5_BLOCK:
            s = S5_BLOCK - 1 - tau
            pb_re[s] = p_re * bb_re - p_im * bb_im
            pb_im[s] = p_re * bb_im + p_im * bb_re
            ere_ref[s] = pb_re[s]
            eim_ref[s] = pb_im[s]
        if tau >= 1:
            fre_ref[tau - 1] = c_re * p_re - c_im * p_im
            fim_ref[tau - 1] = -(c_re * p_im + c_im * p_re)
        if tau == S5_BLOCK:
            pwre_ref[...] = p_re
            pwim_ref[...] = p_im
        p_re, p_im = (p_re * ab_re - p_im * ab_im, p_re * ab_im + p_im * ab_re)

    dn = (((2,), (2,)), ((0,), (0,)))
    krev = (lax.dot_general(c_re, jnp.concatenate(pb_re, axis=1), dn,
                            precision=lax.Precision.HIGHEST, preferred_element_type=F32)
            - lax.dot_general(c_im, jnp.concatenate(pb_im, axis=1), dn,
                              precision=lax.Precision.HIGHEST, preferred_element_type=F32))
    lane = lax.broadcasted_iota(jnp.int32, krev.shape, 2)
    for t in range(S5_BLOCK):
        keep = S5_GROUP * (t + 1)
        moved = pltpu.roll(krev, keep % S5_BLOCK_W, 2)
        mt_ref[:, t * S5_GROUP:(t + 1) * S5_GROUP, :] = jnp.where(
            lane < keep, moved, 0.0).astype(BF16)


def _s5_prep(a_re, a_im, log_dt, b_re, b_im, c_re, c_im):
    g = S5_N_GROUPS
    bt_re = jnp.transpose(b_re, (0, 2, 1))
    bt_im = jnp.transpose(b_im, (0, 2, 1))
    tab = jax.ShapeDtypeStruct((S5_BLOCK, g, S5_GROUP, S5_STATE), F32)
    return pl.pallas_call(
        _s5_prep_kernel,
        out_shape=(jax.ShapeDtypeStruct((g, S5_BLOCK_W, S5_BLOCK_W), BF16), tab, tab, tab, tab,
                   jax.ShapeDtypeStruct((g, 1, S5_STATE), F32),
                   jax.ShapeDtypeStruct((g, 1, S5_STATE), F32)),
        name="s5_prep",
    )(a_re.reshape(g, 1, S5_STATE), a_im.reshape(g, 1, S5_STATE), log_dt.reshape(g, 1, 1),
      bt_re, bt_im, c_re, c_im)


def _in_proj_kernel(x_ref, g_ref, wz_ref, wxbct_ref, wdt_ref, wu_ref, wg_ref,
                    zs_ref, xbct_ref, dt_ref, dtt_ref, u_ref, gate_ref):
    h = (_rms(x_ref[...]) * g_ref[...]).astype(BF16)
    xbct_ref[...] = lax.dot_general(wxbct_ref[...], h, (((1,), (1,)), ((), ())),
                                    preferred_element_type=F32).astype(BF16)
    zs_ref[...] = _silu(jnp.dot(h, wz_ref[...], preferred_element_type=F32)).astype(BF16)
    u = jnp.dot(h, wu_ref[...], preferred_element_type=F32)
    for j in range(u_ref.shape[0]):
        u_ref[j] = u[:, j * V7X_LANES:(j + 1) * V7X_LANES]
    gate_ref[...] = jax.nn.sigmoid(
        jnp.dot(h, wg_ref[...], preferred_element_type=F32)).astype(BF16)
    dt = jnp.dot(h, wdt_ref[...], preferred_element_type=F32)
    dt_ref[...] = dt
    for c in range(dtt_ref.shape[0]):
        dtt_ref[c] = dt[c * SSD_CHUNK:(c + 1) * SSD_CHUNK, :].T[:SSD_N_HEADS, :]


def _in_proj(x2, g_mix, w_in):
    t = x2.shape[0]
    tm = IN_PROJ_ROW_TILE
    wz = w_in[:, OFF_Z:OFF_XBC].astype(BF16)
    wxbct = w_in[:, OFF_XBC:OFF_DT].T.astype(BF16)
    wdt_pad = jnp.pad(w_in[:, OFF_DT:OFF_U],
                      ((0, 0), (0, V7X_LANES - SSD_N_HEADS))).astype(BF16)
    wu = w_in[:, OFF_U:OFF_G].astype(BF16)
    wg = w_in[:, OFF_G:].astype(BF16)
    n_slab = S5_WIDTH // V7X_LANES
    row = lambda w: pl.BlockSpec((tm, w), lambda i: (i, 0))
    return pl.pallas_call(
        _in_proj_kernel,
        grid=(t // tm,),
        in_specs=[row(D_MODEL), _const_spec((1, D_MODEL)), _const_spec(wz.shape),
                  _const_spec(wxbct.shape), _const_spec(wdt_pad.shape),
                  _const_spec(wu.shape), _const_spec(wg.shape)],
        out_specs=[row(SSD_D_INNER),
                   pl.BlockSpec((SSD_CONV_DIM, tm), lambda i: (0, i)),
                   row(V7X_LANES),
                   pl.BlockSpec((tm // SSD_CHUNK, SSD_N_HEADS, SSD_CHUNK), lambda i: (i, 0, 0)),
                   pl.BlockSpec((n_slab, tm, V7X_LANES), lambda i: (0, i, 0)),
                   row(N_BRANCHES * D_MODEL)],
        out_shape=(jax.ShapeDtypeStruct((t, SSD_D_INNER), BF16),
                   jax.ShapeDtypeStruct((SSD_CONV_DIM, t), BF16),
                   jax.ShapeDtypeStruct((t, V7X_LANES), F32),
                   jax.ShapeDtypeStruct((t // SSD_CHUNK, SSD_N_HEADS, SSD_CHUNK), F32),
                   jax.ShapeDtypeStruct((n_slab, t, V7X_LANES), F32),
                   jax.ShapeDtypeStruct((t, N_BRANCHES * D_MODEL), BF16)),
        compiler_params=pltpu.CompilerParams(dimension_semantics=("arbitrary",),
                                             vmem_limit_bytes=V7X_VMEM_LIMIT_BYTES),
        name="in_proj",
    )(x2, g_mix.reshape(1, D_MODEL), wz, wxbct, wdt_pad, wu, wg)


def _ssd_kernel(xbct_ref, zs_ref, dt_ref, dtt_ref, cwl_ref, cbl_ref, dtbc_ref, dtbr_ref,
                ac_ref, ar_ref, dcol_ref, ng_ref, y_ref, xprev, ht_s, st_s):
    tm = xbct_ref.shape[1]
    nch = tm // SSD_CHUNK
    q = SSD_CHUNK

    @pl.when(pl.program_id(1) == 0)
    def _():
        st_s[...] = jnp.zeros_like(st_s)
        xprev[...] = jnp.zeros_like(xprev)

    sh_r = lax.broadcasted_iota(jnp.int32, (2 * q, q), 0)
    sh_c = lax.broadcasted_iota(jnp.int32, (2 * q, q), 1)
    shift_all = jnp.concatenate(
        [(sh_r == sh_c + (q - (SSD_CONV - 1) + k)).astype(BF16) for k in range(SSD_CONV - 1)],
        axis=1)

    def conv_t(c, c0, n):
        if c == 0:
            win = jnp.concatenate([xprev[c0:c0 + n, :], xbct_ref[c0:c0 + n, 0:q]], axis=1)
        else:
            win = xbct_ref[c0:c0 + n, (c - 1) * q:(c + 1) * q]
        sh = jnp.dot(win, shift_all, preferred_element_type=F32)
        acc = cbl_ref[c0:c0 + n, :] + cwl_ref[SSD_CONV - 1, c0:c0 + n, :] * win[:, q:].astype(F32)
        for k in range(SSD_CONV - 1):
            acc = acc + cwl_ref[k, c0:c0 + n, :] * sh[:, k * q:(k + 1) * q]
        return acc + acc * jnp.tanh(acc)

    row_i = lax.broadcasted_iota(jnp.int32, (q, q), 0)
    col_i = lax.broadcasted_iota(jnp.int32, (q, q), 1)
    causal = row_i >= col_i
    tri = causal.astype(BF16)
    tri_t = (row_i <= col_i).astype(BF16)
    head_of_lane = lax.broadcasted_iota(jnp.int32, (q, SSD_GROUP_W), 1) // SSD_HEADDIM
    widen = (lax.broadcasted_iota(jnp.int32, (SSD_N_HEADS, SSD_D_INNER), 1) // SSD_HEADDIM
             == lax.broadcasted_iota(jnp.int32, (SSD_N_HEADS, SSD_D_INNER), 0)).astype(BF16)
    a_c = -jnp.exp(ac_ref[...])
    a_r = -jnp.exp(ar_ref[...])

    def chunk(c):
        r0 = c * q
        for c0 in range(0, SSD_CONV_DIM, CONV_ROWS):
            ht_s[c0:c0 + CONV_ROWS, :] = conv_t(c, c0, CONV_ROWS)
        dt_c = _softplus(dt_ref[pl.ds(r0, q), 0:SSD_N_HEADS] + dtbc_ref[...])
        dt_r = _softplus(dtt_ref[c] + dtbr_ref[...])
        la_c = dt_c * a_c
        la_r = dt_r * a_r
        cum_c = sum(jnp.dot(tri, p, preferred_element_type=F32) for p in _split3(la_c))
        cum_r = sum(jnp.dot(p, tri_t, preferred_element_type=F32) for p in _split3(la_r))
        cum_rl = cum_r - jnp.log(dt_r)
        to_end = dt_c * jnp.exp(cum_c[q - 1:q, :] - cum_c)
        te_x = jnp.dot(to_end.astype(BF16), widen, preferred_element_type=F32)
        ec_x = jnp.dot(jnp.exp(cum_c).astype(BF16), widen, preferred_element_type=F32)

        for g in range(SSD_N_GROUPS):
            b0 = SSD_D_INNER + g * SSD_D_STATE
            c0 = SSD_D_INNER + SSD_BC_W + g * SSD_D_STATE
            bgt = ht_s[b0:b0 + SSD_D_STATE, :].astype(BF16)
            cg = ht_s[c0:c0 + SSD_D_STATE, :].T.astype(BF16)
            cb = jnp.dot(cg, bgt, preferred_element_type=F32)
            ws = []
            for r in range(SSD_HPG):
                h = g * SSD_HPG + r
                cc = jnp.broadcast_to(cum_c[:, h:h + 1], (q, q))
                cr = jnp.broadcast_to(cum_rl[h:h + 1, :], (q, q))
                lm = jnp.exp(jnp.where(causal, cc - cr, -jnp.inf))
                ws.append((cb * lm).astype(BF16))
            wcat = jnp.concatenate(ws, axis=1)
            cols = slice(g * SSD_GROUP_W, (g + 1) * SSD_GROUP_W)
            xs_g = ht_s[g * SSD_GROUP_W:(g + 1) * SSD_GROUP_W, :].T
            xs_b = xs_g.astype(BF16)
            xdtd = (xs_g * te_x[:, cols]).astype(BF16)
            xbd = jnp.concatenate(
                [jnp.where(head_of_lane == r, xs_b, jnp.zeros_like(xs_b))
                 for r in range(SSD_HPG)], axis=0)
            y = jnp.dot(wcat, xbd, preferred_element_type=F32)
            prev = st_s[:, cols]
            y = y + jnp.dot(cg, prev.astype(BF16), preferred_element_type=F32) * ec_x[:, cols]
            st_s[:, cols] = prev * ec_x[q - 1:q, cols] + jnp.dot(
                bgt, xdtd, preferred_element_type=F32)
            y = y + dcol_ref[:, cols] * xs_g
            y = y * zs_ref[pl.ds(r0, q), cols].astype(F32)
            y_ref[pl.ds(r0, q), cols] = (_rms(y) * ng_ref[:, cols]).astype(BF16)

    for c in range(nch):
        chunk(c)
    xprev[...] = xbct_ref[:, tm - q:tm]


def _ssd(xbct, zs, dt, dtt, conv_w, conv_b, dt_bias, a_log, d_ssd, ssd_norm_g, bsz, seqlen):
    tm = SSD_ROW_TILE
    nt = seqlen // tm
    nch = tm // SSD_CHUNK
    h = SSD_N_HEADS
    row = lambda w: pl.BlockSpec((tm, w), lambda b, i: (b * nt + i, 0))
    cwl = jnp.broadcast_to((0.5 * conv_w)[:, :, None], (SSD_CONV, SSD_CONV_DIM, V7X_LANES))
    cbl = jnp.broadcast_to((0.5 * conv_b)[:, None], (SSD_CONV_DIM, V7X_LANES))
    return pl.pallas_call(
        _ssd_kernel,
        grid=(bsz, nt),
        in_specs=[pl.BlockSpec((SSD_CONV_DIM, tm), lambda b, i: (0, b * nt + i)),
                  row(SSD_D_INNER), row(V7X_LANES),
                  pl.BlockSpec((nch, h, SSD_CHUNK), lambda b, i: (b * nt + i, 0, 0)),
                  _const_spec(cwl.shape), _const_spec(cbl.shape),
                  _const_spec((1, h)), _const_spec((h, 1)), _const_spec((1, h)),
                  _const_spec((h, 1)), _const_spec((1, SSD_D_INNER)),
                  _const_spec((1, SSD_D_INNER))],
        out_specs=row(SSD_D_INNER),
        out_shape=jax.ShapeDtypeStruct((bsz * seqlen, SSD_D_INNER), BF16),
        scratch_shapes=[pltpu.VMEM((SSD_CONV_DIM, SSD_CHUNK), BF16),
                        pltpu.VMEM((SSD_CONV_DIM, SSD_CHUNK), F32),
                        pltpu.VMEM((SSD_D_STATE, SSD_D_INNER), F32)],
        compiler_params=pltpu.CompilerParams(dimension_semantics=("arbitrary", "arbitrary"),
                                             vmem_limit_bytes=V7X_VMEM_LIMIT_BYTES),
        name="ssd",
    )(xbct, zs, dt, dtt, cwl, cbl, dt_bias.reshape(1, h), dt_bias.reshape(h, 1),
      a_log.reshape(1, h), a_log.reshape(h, 1),
      jnp.repeat(d_ssd, SSD_HEADDIM).reshape(1, SSD_D_INNER), ssd_norm_g.reshape(1, -1))


def _s5_kernel(u_ref, mt_ref, et_ref, ft_ref, pre_ref, pim_ref, d_ref, y_ref,
               ut_s, yt_s, il_s, sre, sim, cre, cim):
    n_slab = u_ref.shape[0]
    nblk = u_ref.shape[1] // S5_BLOCK
    npair = S5_N_GROUPS // 2
    gps = V7X_LANES // S5_GROUP
    hw = V7X_LANES

    @pl.when(pl.program_id(1) == 0)
    def _():
        cre[...] = jnp.zeros_like(cre)
        cim[...] = jnp.zeros_like(cim)

    sub = S5_IL_STRIDE
    for j in range(n_slab):
        for c in range(sub):
            il_s[j, c] = u_ref[j, pl.ds(c, nblk * sub, stride=sub), :]
    for s in range(S5_BLOCK):
        for j in range(n_slab):
            xt = il_s[j, s % sub, pl.ds(s // sub, nblk, stride=sub), :].astype(BF16).T
            for gl in range(gps):
                ut_s[j * gps + gl, s * S5_GROUP:(s + 1) * S5_GROUP, :] = (
                    xt[gl * S5_GROUP:(gl + 1) * S5_GROUP, :])

    for pr in range(npair):
        ucat = jnp.concatenate([ut_s[2 * pr], ut_s[2 * pr + 1]], axis=0)
        sl = jnp.dot(et_ref[pr], ucat, preferred_element_type=F32).T
        sre[pl.ds(pr, nblk, stride=S5_STATE_PITCH), :] = sl[:, :hw]
        sim[pl.ds(pr, nblk, stride=S5_STATE_PITCH), :] = sl[:, hw:]

    a_re = pre_ref[...]
    a_im = pim_ref[...]

    def step(r, carry):
        c_re, c_im = carry
        rows = pl.ds(pl.multiple_of(r * S5_STATE_PITCH, V7X_SUBLANES), npair)
        l_re = sre[rows, :]
        l_im = sim[rows, :]
        sre[rows, :] = c_re
        sim[rows, :] = c_im
        return (a_re * c_re - a_im * c_im + l_re, a_re * c_im + a_im * c_re + l_im)

    c_re, c_im = lax.fori_loop(0, nblk, step, (cre[...], cim[...]))
    cre[...] = c_re
    cim[...] = c_im

    for pr in range(npair):
        sp = jnp.concatenate([sre[pl.ds(pr, nblk, stride=S5_STATE_PITCH), :],
                              sim[pl.ds(pr, nblk, stride=S5_STATE_PITCH), :]], axis=1).astype(BF16)
        cross = lax.dot_general(ft_ref[pr], sp, (((1,), (1,)), ((), ())),
                                preferred_element_type=F32)
        for j in range(2):
            gi = 2 * pr + j
            ug = ut_s[gi]
            y = (jnp.dot(mt_ref[gi], ug, preferred_element_type=F32)
                 + cross[j * S5_BLOCK_W:(j + 1) * S5_BLOCK_W, :]
                 + d_ref[gi] * ug.astype(F32))
            yt_s[gi] = _gelu_tanh(y).astype(BF16)

    for t in range(S5_BLOCK):
        for j in range(n_slab):
            zt = jnp.concatenate(
                [yt_s[j * gps + gl, t * S5_GROUP:(t + 1) * S5_GROUP, :] for gl in range(gps)],
                axis=0)
            il_s[j, t % sub, pl.ds(t // sub, nblk, stride=sub), :] = zt.T.astype(F32)
    for j in range(n_slab):
        for c in range(sub):
            y_ref[j, pl.ds(c, nblk * sub, stride=sub), :] = il_s[j, c]


def _s5(u_slab, mt, et, ft, pw_re, pw_im, d_col, bsz):
    n_slab, t, _ = u_slab.shape
    nblk = S5_BLOCKS_PER_STEP
    rows = nblk * S5_BLOCK
    parts = t // bsz // rows
    g, w = S5_N_GROUPS, S5_BLOCK_W
    io = pl.BlockSpec((n_slab, rows, V7X_LANES), lambda b, p: (0, b * parts + p, 0))
    state = pltpu.VMEM((nblk * S5_STATE_PITCH, V7X_LANES), F32)
    carry = pltpu.VMEM((g // 2, V7X_LANES), F32)
    return pl.pallas_call(
        _s5_kernel,
        grid=(bsz, parts),
        in_specs=[io, _const_spec(mt.shape), _const_spec(et.shape), _const_spec(ft.shape),
                  _const_spec(pw_re.shape), _const_spec(pw_im.shape), _const_spec(d_col.shape)],
        out_specs=io,
        out_shape=jax.ShapeDtypeStruct(u_slab.shape, F32),
        scratch_shapes=[pltpu.VMEM((g, w, nblk), BF16), pltpu.VMEM((g, w, nblk), BF16),
                        pltpu.VMEM((n_slab, S5_IL_STRIDE, rows // S5_IL_STRIDE, V7X_LANES), F32),
                        state, state, carry, carry],
        compiler_params=pltpu.CompilerParams(dimension_semantics=("arbitrary", "arbitrary"),
                                             vmem_limit_bytes=V7X_VMEM_LIMIT_BYTES),
        name="s5",
    )(u_slab, mt, et, ft, pw_re, pw_im, d_col)


def _s5_tables(mt, e_re, e_im, f_re, f_im, pw_re, pw_im, s5_d):
    g, q, w = S5_N_GROUPS, S5_BLOCK, S5_BLOCK_W
    eg_re = jnp.transpose(e_re, (1, 0, 2, 3)).reshape(g // 2, 2, w, S5_STATE)
    eg_im = jnp.transpose(e_im, (1, 0, 2, 3)).reshape(g // 2, 2, w, S5_STATE)
    zero = jnp.zeros_like(eg_re[:, 0])
    e_top = jnp.concatenate([eg_re[:, 0], zero, eg_im[:, 0], zero], axis=-1)
    e_bot = jnp.concatenate([zero, eg_re[:, 1], zero, eg_im[:, 1]], axis=-1)
    e = jnp.concatenate([e_top, e_bot], axis=1)
    fg_re = jnp.transpose(f_re, (1, 3, 0, 2)).reshape(g // 2, 2, S5_STATE, w)
    fg_im = jnp.transpose(f_im, (1, 3, 0, 2)).reshape(g // 2, 2, S5_STATE, w)
    zf = jnp.zeros_like(fg_re[:, 0])
    f = jnp.concatenate([
        jnp.concatenate([fg_re[:, 0], zf], axis=-1),
        jnp.concatenate([zf, fg_re[:, 1]], axis=-1),
        jnp.concatenate([fg_im[:, 0], zf], axis=-1),
        jnp.concatenate([zf, fg_im[:, 1]], axis=-1)], axis=1)
    tr = lambda a: jnp.swapaxes(a, 1, 2).astype(BF16)
    pw_re = pw_re.reshape(g // 2, 2 * S5_STATE)
    pw_im = pw_im.reshape(g // 2, 2 * S5_STATE)
    d_col = jnp.tile(s5_d.reshape(g, 1, S5_GROUP), (1, q, 1)).reshape(g, w, 1)
    return mt, tr(e), tr(f), pw_re, pw_im, d_col


def _merge_mlp_kernel(x_ref, ya_ref, yb_ref, gate_ref, gw_ref, gb_ref, wa_ref, wb_ref, wo_ref,
                      g2_ref, wi_ref, wo2_ref, g3_ref, o_ref):
    ybf = jnp.concatenate([yb_ref[j] for j in range(yb_ref.shape[0])], axis=1)
    yb = ybf.astype(BF16)
    glu = jnp.dot(yb, gw_ref[...], preferred_element_type=F32) + gb_ref[...]
    ybg = (ybf * jax.nn.sigmoid(glu)).astype(BF16)
    pa = jnp.dot(ya_ref[...], wa_ref[...], preferred_element_type=F32)
    pb = jnp.dot(ybg, wb_ref[...], preferred_element_type=F32)
    merged = (gate_ref[:, :D_MODEL].astype(F32) * pa
              + gate_ref[:, D_MODEL:].astype(F32) * pb).astype(BF16)
    x1 = x_ref[...] + jnp.dot(merged, wo_ref[...], preferred_element_type=F32)
    h2 = (_rms(x1) * g2_ref[...]).astype(BF16)
    acc = x1
    for k in range(D_FF // D_MODEL):
        cols = slice(k * D_MODEL, (k + 1) * D_MODEL)
        hid = jnp.dot(h2, wi_ref[:, cols], preferred_element_type=F32)
        hid = jnp.square(jnp.maximum(hid, 0.0)).astype(BF16)
        acc = acc + jnp.dot(hid, wo2_ref[cols, :], preferred_element_type=F32)
    o_ref[...] = _rms(acc) * g3_ref[...]


def _merge_mlp(x2, ya, yb_slab, gates, glu_w, glu_b, w_branch, w_out, g_mlp, w_mlp_in,
               w_mlp_out, g_final):
    t = x2.shape[0]
    tm = ROW_TILE
    wa = w_branch[:SSD_D_INNER].astype(BF16)
    wb = w_branch[SSD_D_INNER:].astype(BF16)
    row = lambda w: pl.BlockSpec((tm, w), lambda i: (i, 0))
    vec = lambda v: v.reshape(1, -1)
    return pl.pallas_call(
        _merge_mlp_kernel,
        grid=(t // tm,),
        in_specs=[row(D_MODEL), row(SSD_D_INNER),
                  pl.BlockSpec((yb_slab.shape[0], tm, V7X_LANES), lambda i: (0, i, 0)),
                  row(N_BRANCHES * D_MODEL),
                  _const_spec((S5_WIDTH, S5_WIDTH)), _const_spec((1, S5_WIDTH)),
                  _const_spec(wa.shape), _const_spec(wb.shape), _const_spec((D_MODEL, D_MODEL)),
                  _const_spec((1, D_MODEL)), _const_spec((D_MODEL, D_FF)),
                  _const_spec((D_FF, D_MODEL)), _const_spec((1, D_MODEL))],
        out_specs=row(D_MODEL),
        out_shape=jax.ShapeDtypeStruct((t, D_MODEL), F32),
        compiler_params=pltpu.CompilerParams(dimension_semantics=("arbitrary",),
                                             vmem_limit_bytes=V7X_VMEM_LIMIT_BYTES),
        name="merge_mlp",
    )(x2, ya, yb_slab, gates, glu_w.astype(BF16), vec(glu_b), wa, wb, w_out.astype(BF16),
      vec(g_mlp), w_mlp_in.astype(BF16), w_mlp_out.astype(BF16), vec(g_final))


def kernel(x, norm_mix_g, w_in, conv_w, conv_b, dt_bias, a_log, d_ssd, ssd_norm_g, s5_a_re, s5_a_im, s5_log_dt, s5_b_re, s5_b_im, s5_c_re, s5_c_im, s5_d, s5_glu_w, s5_glu_b, w_branch, w_out, norm_mlp_g, w_mlp_in, w_mlp_out, norm_final_g):
    bsz, seqlen, _ = x.shape
    t = bsz * seqlen
    assert seqlen % SSD_ROW_TILE == 0 and t % ROW_TILE == 0 and t % IN_PROJ_ROW_TILE == 0
    assert seqlen % (S5_BLOCKS_PER_STEP * S5_BLOCK) == 0
    x2 = x.reshape(t, D_MODEL)

    zs, xbct, dt, dtt, u_slab, gates = _in_proj(x2, norm_mix_g, w_in)
    ya = _ssd(xbct, zs, dt, dtt, conv_w, conv_b, dt_bias, a_log, d_ssd, ssd_norm_g, bsz,
              seqlen)
    tables = _s5_tables(*_s5_prep(s5_a_re, s5_a_im, s5_log_dt, s5_b_re, s5_b_im,
                                  s5_c_re, s5_c_im), s5_d)
    yb_slab = _s5(u_slab, *tables, bsz)
    out = _merge_mlp(x2, ya, yb_slab, gates, s5_glu_w, s5_glu_b, w_branch, w_out, norm_mlp_g,
                     w_mlp_in, w_mlp_out, norm_final_g)
    return out.reshape(bsz, seqlen, D_MODEL)
```

```python
import jax
import jax.numpy as jnp
from jax import lax
from jax.experimental import pallas as pl
from jax.experimental.pallas import tpu as pltpu

F32 = jnp.float32
BF16 = jnp.bfloat16

D_MODEL = 1024
SSD_D_INNER = D_MODEL
SSD_HEADDIM = 64
SSD_N_HEADS = SSD_D_INNER // SSD_HEADDIM
SSD_N_GROUPS = 4
SSD_HPG = SSD_N_HEADS // SSD_N_GROUPS
SSD_D_STATE = 128
SSD_CONV = 4
SSD_CHUNK = 128
SSD_GROUP_W = SSD_D_INNER // SSD_N_GROUPS
SSD_BC_W = SSD_N_GROUPS * SSD_D_STATE
SSD_CONV_DIM = SSD_D_INNER + 2 * SSD_BC_W
S5_WIDTH = D_MODEL // 2
S5_GROUP = 16
S5_N_GROUPS = S5_WIDTH // S5_GROUP
S5_STATE = 64
S5_BLOCK = 16
S5_BLOCK_W = S5_BLOCK * S5_GROUP
D_FF = 4 * D_MODEL
N_BRANCHES = 2
OFF_Z = 0
OFF_XBC = OFF_Z + SSD_D_INNER
OFF_DT = OFF_XBC + SSD_CONV_DIM
OFF_U = OFF_DT + SSD_N_HEADS
OFF_G = OFF_U + S5_WIDTH
EPS = 1e-6

V7X_LANES = 128
V7X_SUBLANES = 8
V7X_VMEM_LIMIT_BYTES = 56 * 1024 * 1024

ROW_TILE = 512
IN_PROJ_ROW_TILE = 1024
IN_PROJ_VMEM_LIMIT_BYTES = 60 * 1024 * 1024
SSD_ROW_TILE = 1024
S5_BLOCKS_PER_STEP = 128
CONV_ROWS = 512
S5_STATE_PITCH = 24
S5_IL_STRIDE = 4


def _const_spec(shape):
    zeros = (0,) * len(shape)
    return pl.BlockSpec(shape, lambda *_: zeros, pipeline_mode=pl.Buffered(1))


def _rms(x):
    return x * lax.rsqrt(jnp.mean(x * x, axis=-1, keepdims=True) + EPS)


def _split3(x):
    hi = x.astype(BF16)
    r1 = x - hi.astype(F32)
    mid = r1.astype(BF16)
    lo = (r1 - mid.astype(F32)).astype(BF16)
    return hi, mid, lo


def _softplus(x):
    return jnp.maximum(x, 0.0) + jnp.log(1.0 + jnp.exp(-jnp.abs(x)))


GELU_C0 = 0.7978845608028654
GELU_C1 = GELU_C0 * 0.044715


def _gelu_tanh(x):
    h = 0.5 * x
    return h + h * jnp.tanh(x * (GELU_C0 + GELU_C1 * (x * x)))


def _silu(x):
    h = 0.5 * x
    return h + h * jnp.tanh(h)


def _s5_prep_kernel(are_ref, aim_ref, ldt_ref, bre_ref, bim_ref, cre_ref, cim_ref,
                    mt_ref, ere_ref, eim_ref, fre_ref, fim_ref, pwre_ref, pwim_ref):
    a_re = are_ref[...]
    a_im = aim_ref[...]
    dt = jnp.exp(ldt_ref[...])
    mag = jnp.exp(a_re * dt)
    ab_re = mag * jnp.cos(a_im * dt)
    ab_im = mag * jnp.sin(a_im * dt)
    den = a_re * a_re + a_im * a_im
    nr = ab_re - 1.0
    ni = ab_im
    coef_re = (nr * a_re + ni * a_im) / den
    coef_im = (ni * a_re - nr * a_im) / den
    b_re = bre_ref[...]
    b_im = bim_ref[...]
    bb_re = coef_re * b_re - coef_im * b_im
    bb_im = coef_re * b_im + coef_im * b_re
    c_re = cre_ref[...]
    c_im = cim_ref[...]

    p_re = jnp.ones_like(ab_re)
    p_im = jnp.zeros_like(ab_re)
    pb_re = [None] * S5_BLOCK
    pb_im = [None] * S5_BLOCK
    for tau in range(S5_BLOCK + 1):
        if tau < S5_BLOCK:
            s = S5_BLOCK - 1 - tau
            pb_re[s] = p_re * bb_re - p_im * bb_im
            pb_im[s] = p_re * bb_im + p_im * bb_re
            ere_ref[s] = pb_re[s]
            eim_ref[s] = pb_im[s]
        if tau >= 1:
            fre_ref[tau - 1] = c_re * p_re - c_im * p_im
            fim_ref[tau - 1] = -(c_re * p_im + c_im * p_re)
        if tau == S5_BLOCK:
            pwre_ref[...] = p_re
            pwim_ref[...] = p_im
        p_re, p_im = (p_re * ab_re - p_im * ab_im, p_re * ab_im + p_im * ab_re)

    dn = (((2,), (2,)), ((0,), (0,)))
    krev = (lax.dot_general(c_re, jnp.concatenate(pb_re, axis=1), dn,
                            precision=lax.Precision.HIGHEST, preferred_element_type=F32)
            - lax.dot_general(c_im, jnp.concatenate(pb_im, axis=1), dn,
                              precision=lax.Precision.HIGHEST, preferred_element_type=F32))
    lane = lax.broadcasted_iota(jnp.int32, krev.shape, 2)
    for t in range(S5_BLOCK):
        keep = S5_GROUP * (t + 1)
        moved = pltpu.roll(krev, keep % S5_BLOCK_W, 2)
        mt_ref[:, t * S5_GROUP:(t + 1) * S5_GROUP, :] = jnp.where(
            lane < keep, moved, 0.0).astype(BF16)


def _s5_prep(a_re, a_im, log_dt, b_re, b_im, c_re, c_im):
    g = S5_N_GROUPS
    bt_re = jnp.transpose(b_re, (0, 2, 1))
    bt_im = jnp.transpose(b_im, (0, 2, 1))
    tab = jax.ShapeDtypeStruct((S5_BLOCK, g, S5_GROUP, S5_STATE), F32)
    return pl.pallas_call(
        _s5_prep_kernel,
        out_shape=(jax.ShapeDtypeStruct((g, S5_BLOCK_W, S5_BLOCK_W), BF16), tab, tab, tab, tab,
                   jax.ShapeDtypeStruct((g, 1, S5_STATE), F32),
                   jax.ShapeDtypeStruct((g, 1, S5_STATE), F32)),
        name="s5_prep",
    )(a_re.reshape(g, 1, S5_STATE), a_im.reshape(g, 1, S5_STATE), log_dt.reshape(g, 1, 1),
      bt_re, bt_im, c_re, c_im)


def _in_proj_kernel(x_ref, g_ref, wz_ref, wxbct_ref, wdt_ref, wu_ref, wg_ref,
                    zs_ref, xbct_ref, dt_ref, dtt_ref, u_ref, gate_ref, wt_s):
    @pl.when(pl.program_id(0) == 0)
    def _():
        wt_s[...] = wxbct_ref[...].T

    h = (_rms(x_ref[...]) * g_ref[...]).astype(BF16)
    xbct_ref[...] = lax.dot_general(wt_s[...], h, (((1,), (1,)), ((), ())),
                                    preferred_element_type=F32).astype(BF16)
    zs_ref[...] = _silu(jnp.dot(h, wz_ref[...], preferred_element_type=F32)).astype(BF16)
    u = jnp.dot(h, wu_ref[...], preferred_element_type=F32)
    for j in range(u_ref.shape[0]):
        u_ref[j] = u[:, j * V7X_LANES:(j + 1) * V7X_LANES]
    gate_ref[...] = jax.nn.sigmoid(
        jnp.dot(h, wg_ref[...], preferred_element_type=F32)).astype(BF16)
    dt = jnp.dot(h, wdt_ref[...], preferred_element_type=F32)
    dt_ref[...] = dt
    for c in range(dtt_ref.shape[0]):
        dtt_ref[c] = dt[c * SSD_CHUNK:(c + 1) * SSD_CHUNK, :].T[:SSD_N_HEADS, :]


def _in_proj(x2, g_mix, w_in):
    t = x2.shape[0]
    tm = IN_PROJ_ROW_TILE
    wz = w_in[:, OFF_Z:OFF_XBC].astype(BF16)
    wxbct = w_in[:, OFF_XBC:OFF_DT].astype(BF16)
    wdt_pad = jnp.pad(w_in[:, OFF_DT:OFF_U],
                      ((0, 0), (0, V7X_LANES - SSD_N_HEADS))).astype(BF16)
    wu = w_in[:, OFF_U:OFF_G].astype(BF16)
    wg = w_in[:, OFF_G:].astype(BF16)
    n_slab = S5_WIDTH // V7X_LANES
    row = lambda w: pl.BlockSpec((tm, w), lambda i: (i, 0))
    return pl.pallas_call(
        _in_proj_kernel,
        grid=(t // tm,),
        in_specs=[row(D_MODEL), _const_spec((1, D_MODEL)), _const_spec(wz.shape),
                  _const_spec(wxbct.shape), _const_spec(wdt_pad.shape),
                  _const_spec(wu.shape), _const_spec(wg.shape)],
        out_specs=[row(SSD_D_INNER),
                   pl.BlockSpec((SSD_CONV_DIM, tm), lambda i: (0, i)),
                   row(V7X_LANES),
                   pl.BlockSpec((tm // SSD_CHUNK, SSD_N_HEADS, SSD_CHUNK), lambda i: (i, 0, 0)),
                   pl.BlockSpec((n_slab, tm, V7X_LANES), lambda i: (0, i, 0)),
                   row(N_BRANCHES * D_MODEL)],
        out_shape=(jax.ShapeDtypeStruct((t, SSD_D_INNER), BF16),
                   jax.ShapeDtypeStruct((SSD_CONV_DIM, t), BF16),
                   jax.ShapeDtypeStruct((t, V7X_LANES), F32),
                   jax.ShapeDtypeStruct((t // SSD_CHUNK, SSD_N_HEADS, SSD_CHUNK), F32),
                   jax.ShapeDtypeStruct((n_slab, t, V7X_LANES), F32),
                   jax.ShapeDtypeStruct((t, N_BRANCHES * D_MODEL), BF16)),
        scratch_shapes=[pltpu.VMEM((SSD_CONV_DIM, D_MODEL), BF16)],
        compiler_params=pltpu.CompilerParams(dimension_semantics=("arbitrary",),
                                             vmem_limit_bytes=IN_PROJ_VMEM_LIMIT_BYTES),
        name="in_proj",
    )(x2, g_mix.reshape(1, D_MODEL), wz, wxbct, wdt_pad, wu, wg)


def _ssd_kernel(xbct_ref, zs_ref, dt_ref, dtt_ref, cwl_ref, cbl_ref, dtbc_ref, dtbr_ref,
                ac_ref, ar_ref, dcol_ref, ng_ref, y_ref, xprev, ht_s, st_s):
    tm = xbct_ref.shape[1]
    nch = tm // SSD_CHUNK
    q = SSD_CHUNK

    @pl.when(pl.program_id(1) == 0)
    def _():
        st_s[...] = jnp.zeros_like(st_s)
        xprev[...] = jnp.zeros_like(xprev)

    sh_r = lax.broadcasted_iota(jnp.int32, (2 * q, q), 0)
    sh_c = lax.broadcasted_iota(jnp.int32, (2 * q, q), 1)
    shift_all = jnp.concatenate(
        [(sh_r == sh_c + (q - (SSD_CONV - 1) + k)).astype(BF16) for k in range(SSD_CONV - 1)],
        axis=1)

    def conv_t(c, c0, n):
        if c == 0:
            win = jnp.concatenate([xprev[c0:c0 + n, :], xbct_ref[c0:c0 + n, 0:q]], axis=1)
        else:
            win = xbct_ref[c0:c0 + n, (c - 1) * q:(c + 1) * q]
        sh = jnp.dot(win, shift_all, preferred_element_type=F32)
        acc = cbl_ref[c0:c0 + n, :] + cwl_ref[SSD_CONV - 1, c0:c0 + n, :] * win[:, q:].astype(F32)
        for k in range(SSD_CONV - 1):
            acc = acc + cwl_ref[k, c0:c0 + n, :] * sh[:, k * q:(k + 1) * q]
        return acc + acc * jnp.tanh(acc)

    row_i = lax.broadcasted_iota(jnp.int32, (q, q), 0)
    col_i = lax.broadcasted_iota(jnp.int32, (q, q), 1)
    causal = row_i >= col_i
    tri = causal.astype(BF16)
    tri_t = (row_i <= col_i).astype(BF16)
    head_of_lane = lax.broadcasted_iota(jnp.int32, (q, SSD_GROUP_W), 1) // SSD_HEADDIM
    widen = (lax.broadcasted_iota(jnp.int32, (SSD_N_HEADS, SSD_D_INNER), 1) // SSD_HEADDIM
             == lax.broadcasted_iota(jnp.int32, (SSD_N_HEADS, SSD_D_INNER), 0)).astype(BF16)
    a_c = -jnp.exp(ac_ref[...])
    a_r = -jnp.exp(ar_ref[...])

    def chunk(c):
        r0 = c * q
        for c0 in range(0, SSD_CONV_DIM, CONV_ROWS):
            ht_s[c0:c0 + CONV_ROWS, :] = conv_t(c, c0, CONV_ROWS)
        dt_c = _softplus(dt_ref[pl.ds(r0, q), 0:SSD_N_HEADS] + dtbc_ref[...])
        dt_r = _softplus(dtt_ref[c] + dtbr_ref[...])
        la_c = dt_c * a_c
        la_r = dt_r * a_r
        cum_c = sum(jnp.dot(tri, p, preferred_element_type=F32) for p in _split3(la_c))
        cum_r = sum(jnp.dot(p, tri_t, preferred_element_type=F32) for p in _split3(la_r))
        cum_rl = cum_r - jnp.log(dt_r)
        to_end = dt_c * jnp.exp(cum_c[q - 1:q, :] - cum_c)
        te_x = jnp.dot(to_end.astype(BF16), widen, preferred_element_type=F32)
        ec_x = jnp.dot(jnp.exp(cum_c).astype(BF16), widen, preferred_element_type=F32)

        for g in range(SSD_N_GROUPS):
            b0 = SSD_D_INNER + g * SSD_D_STATE
            c0 = SSD_D_INNER + SSD_BC_W + g * SSD_D_STATE
            bgt = ht_s[b0:b0 + SSD_D_STATE, :].astype(BF16)
            cg = ht_s[c0:c0 + SSD_D_STATE, :].T.astype(BF16)
            cb = jnp.dot(cg, bgt, preferred_element_type=F32)
            ws = []
            for r in range(SSD_HPG):
                h = g * SSD_HPG + r
                cc = jnp.broadcast_to(cum_c[:, h:h + 1], (q, q))
                cr = jnp.broadcast_to(cum_rl[h:h + 1, :], (q, q))
                lm = jnp.exp(jnp.where(causal, cc - cr, -jnp.inf))
                ws.append((cb * lm).astype(BF16))
            wcat = jnp.concatenate(ws, axis=1)
            cols = slice(g * SSD_GROUP_W, (g + 1) * SSD_GROUP_W)
            xs_g = ht_s[g * SSD_GROUP_W:(g + 1) * SSD_GROUP_W, :].T
            xs_b = xs_g.astype(BF16)
            xdtd = (xs_g * te_x[:, cols]).astype(BF16)
            xbd = jnp.concatenate(
                [jnp.where(head_of_lane == r, xs_b, jnp.zeros_like(xs_b))
                 for r in range(SSD_HPG)], axis=0)
            y = jnp.dot(wcat, xbd, preferred_element_type=F32)
            prev = st_s[:, cols]
            y = y + jnp.dot(cg, prev.astype(BF16), preferred_element_type=F32) * ec_x[:, cols]
            st_s[:, cols] = prev * ec_x[q - 1:q, cols] + jnp.dot(
                bgt, xdtd, preferred_element_type=F32)
            y = y + dcol_ref[:, cols] * xs_g
            y = y * zs_ref[pl.ds(r0, q), cols].astype(F32)
            y_ref[pl.ds(r0, q), cols] = (_rms(y) * ng_ref[:, cols]).astype(BF16)

    for c in range(nch):
        chunk(c)
    xprev[...] = xbct_ref[:, tm - q:tm]


def _ssd(xbct, zs, dt, dtt, conv_w, conv_b, dt_bias, a_log, d_ssd, ssd_norm_g, bsz, seqlen):
    tm = SSD_ROW_TILE
    nt = seqlen // tm
    nch = tm // SSD_CHUNK
    h = SSD_N_HEADS
    row = lambda w: pl.BlockSpec((tm, w), lambda b, i: (b * nt + i, 0))
    cwl = jnp.broadcast_to((0.5 * conv_w)[:, :, None], (SSD_CONV, SSD_CONV_DIM, V7X_LANES))
    cbl = jnp.broadcast_to((0.5 * conv_b)[:, None], (SSD_CONV_DIM, V7X_LANES))
    return pl.pallas_call(
        _ssd_kernel,
        grid=(bsz, nt),
        in_specs=[pl.BlockSpec((SSD_CONV_DIM, tm), lambda b, i: (0, b * nt + i)),
                  row(SSD_D_INNER), row(V7X_LANES),
                  pl.BlockSpec((nch, h, SSD_CHUNK), lambda b, i: (b * nt + i, 0, 0)),
                  _const_spec(cwl.shape), _const_spec(cbl.shape),
                  _const_spec((1, h)), _const_spec((h, 1)), _const_spec((1, h)),
                  _const_spec((h, 1)), _const_spec((1, SSD_D_INNER)),
                  _const_spec((1, SSD_D_INNER))],
        out_specs=row(SSD_D_INNER),
        out_shape=jax.ShapeDtypeStruct((bsz * seqlen, SSD_D_INNER), BF16),
        scratch_shapes=[pltpu.VMEM((SSD_CONV_DIM, SSD_CHUNK), BF16),
                        pltpu.VMEM((SSD_CONV_DIM, SSD_CHUNK), F32),
                        pltpu.VMEM((SSD_D_STATE, SSD_D_INNER), F32)],
        compiler_params=pltpu.CompilerParams(dimension_semantics=("arbitrary", "arbitrary"),
                                             vmem_limit_bytes=V7X_VMEM_LIMIT_BYTES),
        name="ssd",
    )(xbct, zs, dt, dtt, cwl, cbl, dt_bias.reshape(1, h), dt_bias.reshape(h, 1),
      a_log.reshape(1, h), a_log.reshape(h, 1),
      jnp.repeat(d_ssd, SSD_HEADDIM).reshape(1, SSD_D_INNER), ssd_norm_g.reshape(1, -1))


def _s5_kernel(u_ref, mt_ref, et_ref, ft_ref, pre_ref, pim_ref, d_ref, y_ref,
               ut_s, yt_s, il_s, sre, sim, cre, cim):
    n_slab = u_ref.shape[0]
    nblk = u_ref.shape[1] // S5_BLOCK
    npair = S5_N_GROUPS // 2
    gps = V7X_LANES // S5_GROUP
    hw = V7X_LANES

    @pl.when(pl.program_id(1) == 0)
    def _():
        cre[...] = jnp.zeros_like(cre)
        cim[...] = jnp.zeros_like(cim)

    sub = S5_IL_STRIDE
    for j in range(n_slab):
        for c in range(sub):
            il_s[j, c] = u_ref[j, pl.ds(c, nblk * sub, stride=sub), :]
    for s in range(S5_BLOCK):
        for j in range(n_slab):
            xt = il_s[j, s % sub, pl.ds(s // sub, nblk, stride=sub), :].astype(BF16).T
            for gl in range(gps):
                ut_s[j * gps + gl, s * S5_GROUP:(s + 1) * S5_GROUP, :] = (
                    xt[gl * S5_GROUP:(gl + 1) * S5_GROUP, :])

    for pr in range(npair):
        ucat = jnp.concatenate([ut_s[2 * pr], ut_s[2 * pr + 1]], axis=0)
        sl = jnp.dot(et_ref[pr], ucat, preferred_element_type=F32).T
        sre[pl.ds(pr, nblk, stride=S5_STATE_PITCH), :] = sl[:, :hw]
        sim[pl.ds(pr, nblk, stride=S5_STATE_PITCH), :] = sl[:, hw:]

    a_re = pre_ref[...]
    a_im = pim_ref[...]

    def step(r, carry):
        c_re, c_im = carry
        rows = pl.ds(pl.multiple_of(r * S5_STATE_PITCH, V7X_SUBLANES), npair)
        l_re = sre[rows, :]
        l_im = sim[rows, :]
        sre[rows, :] = c_re
        sim[rows, :] = c_im
        return (a_re * c_re - a_im * c_im + l_re, a_re * c_im + a_im * c_re + l_im)

    c_re, c_im = lax.fori_loop(0, nblk, step, (cre[...], cim[...]))
    cre[...] = c_re
    cim[...] = c_im

    for pr in range(npair):
        sp = jnp.concatenate([sre[pl.ds(pr, nblk, stride=S5_STATE_PITCH), :],
                              sim[pl.ds(pr, nblk, stride=S5_STATE_PITCH), :]], axis=1).astype(BF16)
        cross = lax.dot_general(ft_ref[pr], sp, (((1,), (1,)), ((), ())),
                                preferred_element_type=F32)
        for j in range(2):
            gi = 2 * pr + j
            ug = ut_s[gi]
            y = (jnp.dot(mt_ref[gi], ug, preferred_element_type=F32)
                 + cross[j * S5_BLOCK_W:(j + 1) * S5_BLOCK_W, :]
                 + d_ref[gi] * ug.astype(F32))
            yt_s[gi] = _gelu_tanh(y).astype(BF16)

    for t in range(S5_BLOCK):
        for j in range(n_slab):
            zt = jnp.concatenate(
                [yt_s[j * gps + gl, t * S5_GROUP:(t + 1) * S5_GROUP, :] for gl in range(gps)],
                axis=0)
            il_s[j, t % sub, pl.ds(t // sub, nblk, stride=sub), :] = zt.T.astype(F32)
    for j in range(n_slab):
        for c in range(sub):
            y_ref[j, pl.ds(c, nblk * sub, stride=sub), :] = il_s[j, c]


def _s5(u_slab, mt, et, ft, pw_re, pw_im, d_col, bsz):
    n_slab, t, _ = u_slab.shape
    nblk = S5_BLOCKS_PER_STEP
    rows = nblk * S5_BLOCK
    parts = t // bsz // rows
    g, w = S5_N_GROUPS, S5_BLOCK_W
    io = pl.BlockSpec((n_slab, rows, V7X_LANES), lambda b, p: (0, b * parts + p, 0))
    state = pltpu.VMEM((nblk * S5_STATE_PITCH, V7X_LANES), F32)
    carry = pltpu.VMEM((g // 2, V7X_LANES), F32)
    return pl.pallas_call(
        _s5_kernel,
        grid=(bsz, parts),
        in_specs=[io, _const_spec(mt.shape), _const_spec(et.shape), _const_spec(ft.shape),
                  _const_spec(pw_re.shape), _const_spec(pw_im.shape), _const_spec(d_col.shape)],
        out_specs=io,
        out_shape=jax.ShapeDtypeStruct(u_slab.shape, F32),
        scratch_shapes=[pltpu.VMEM((g, w, nblk), BF16), pltpu.VMEM((g, w, nblk), BF16),
                        pltpu.VMEM((n_slab, S5_IL_STRIDE, rows // S5_IL_STRIDE, V7X_LANES), F32),
                        state, state, carry, carry],
        compiler_params=pltpu.CompilerParams(dimension_semantics=("arbitrary", "arbitrary"),
                                             vmem_limit_bytes=V7X_VMEM_LIMIT_BYTES),
        name="s5",
    )(u_slab, mt, et, ft, pw_re, pw_im, d_col)


def _s5_tables(mt, e_re, e_im, f_re, f_im, pw_re, pw_im, s5_d):
    g, q, w = S5_N_GROUPS, S5_BLOCK, S5_BLOCK_W
    eg_re = jnp.transpose(e_re, (1, 0, 2, 3)).reshape(g // 2, 2, w, S5_STATE)
    eg_im = jnp.transpose(e_im, (1, 0, 2, 3)).reshape(g // 2, 2, w, S5_STATE)
    zero = jnp.zeros_like(eg_re[:, 0])
    e_top = jnp.concatenate([eg_re[:, 0], zero, eg_im[:, 0], zero], axis=-1)
    e_bot = jnp.concatenate([zero, eg_re[:, 1], zero, eg_im[:, 1]], axis=-1)
    e = jnp.concatenate([e_top, e_bot], axis=1)
    fg_re = jnp.transpose(f_re, (1, 3, 0, 2)).reshape(g // 2, 2, S5_STATE, w)
    fg_im = jnp.transpose(f_im, (1, 3, 0, 2)).reshape(g // 2, 2, S5_STATE, w)
    zf = jnp.zeros_like(fg_re[:, 0])
    f = jnp.concatenate([
        jnp.concatenate([fg_re[:, 0], zf], axis=-1),
        jnp.concatenate([zf, fg_re[:, 1]], axis=-1),
        jnp.concatenate([fg_im[:, 0], zf], axis=-1),
        jnp.concatenate([zf, fg_im[:, 1]], axis=-1)], axis=1)
    tr = lambda a: jnp.swapaxes(a, 1, 2).astype(BF16)
    pw_re = pw_re.reshape(g // 2, 2 * S5_STATE)
    pw_im = pw_im.reshape(g // 2, 2 * S5_STATE)
    d_col = jnp.tile(s5_d.reshape(g, 1, S5_GROUP), (1, q, 1)).reshape(g, w, 1)
    return mt, tr(e), tr(f), pw_re, pw_im, d_col


def _merge_mlp_kernel(x_ref, ya_ref, yb_ref, gate_ref, gw_ref, gb_ref, wa_ref, wb_ref, wo_ref,
                      g2_ref, wi_ref, wo2_ref, g3_ref, o_ref):
    ybf = jnp.concatenate([yb_ref[j] for j in range(yb_ref.shape[0])], axis=1)
    yb = ybf.astype(BF16)
    glu = jnp.dot(yb, gw_ref[...], preferred_element_type=F32) + gb_ref[...]
    ybg = (ybf * jax.nn.sigmoid(glu)).astype(BF16)
    pa = jnp.dot(ya_ref[...], wa_ref[...], preferred_element_type=F32)
    pb = jnp.dot(ybg, wb_ref[...], preferred_element_type=F32)
    merged = (gate_ref[:, :D_MODEL].astype(F32) * pa
              + gate_ref[:, D_MODEL:].astype(F32) * pb).astype(BF16)
    x1 = x_ref[...] + jnp.dot(merged, wo_ref[...], preferred_element_type=F32)
    h2 = (_rms(x1) * g2_ref[...]).astype(BF16)
    acc = x1
    for k in range(D_FF // D_MODEL):
        cols = slice(k * D_MODEL, (k + 1) * D_MODEL)
        hid = jnp.dot(h2, wi_ref[:, cols], preferred_element_type=F32)
        hid = jnp.square(jnp.maximum(hid, 0.0)).astype(BF16)
        acc = acc + jnp.dot(hid, wo2_ref[cols, :], preferred_element_type=F32)
    o_ref[...] = _rms(acc) * g3_ref[...]


def _merge_mlp(x2, ya, yb_slab, gates, glu_w, glu_b, w_branch, w_out, g_mlp, w_mlp_in,
               w_mlp_out, g_final):
    t = x2.shape[0]
    tm = ROW_TILE
    wa = w_branch[:SSD_D_INNER].astype(BF16)
    wb = w_branch[SSD_D_INNER:].astype(BF16)
    row = lambda w: pl.BlockSpec((tm, w), lambda i: (i, 0))
    vec = lambda v: v.reshape(1, -1)
    return pl.pallas_call(
        _merge_mlp_kernel,
        grid=(t // tm,),
        in_specs=[row(D_MODEL), row(SSD_D_INNER),
                  pl.BlockSpec((yb_slab.shape[0], tm, V7X_LANES), lambda i: (0, i, 0)),
                  row(N_BRANCHES * D_MODEL),
                  _const_spec((S5_WIDTH, S5_WIDTH)), _const_spec((1, S5_WIDTH)),
                  _const_spec(wa.shape), _const_spec(wb.shape), _const_spec((D_MODEL, D_MODEL)),
                  _const_spec((1, D_MODEL)), _const_spec((D_MODEL, D_FF)),
                  _const_spec((D_FF, D_MODEL)), _const_spec((1, D_MODEL))],
        out_specs=row(D_MODEL),
        out_shape=jax.ShapeDtypeStruct((t, D_MODEL), F32),
        compiler_params=pltpu.CompilerParams(dimension_semantics=("arbitrary",),
                                             vmem_limit_bytes=V7X_VMEM_LIMIT_BYTES),
        name="merge_mlp",
    )(x2, ya, yb_slab, gates, glu_w.astype(BF16), vec(glu_b), wa, wb, w_out.astype(BF16),
      vec(g_mlp), w_mlp_in.astype(BF16), w_mlp_out.astype(BF16), vec(g_final))


def kernel(x, norm_mix_g, w_in, conv_w, conv_b, dt_bias, a_log, d_ssd, ssd_norm_g, s5_a_re, s5_a_im, s5_log_dt, s5_b_re, s5_b_im, s5_c_re, s5_c_im, s5_d, s5_glu_w, s5_glu_b, w_branch, w_out, norm_mlp_g, w_mlp_in, w_mlp_out, norm_final_g):
    bsz, seqlen, _ = x.shape
    t = bsz * seqlen
    assert seqlen % SSD_ROW_TILE == 0 and t % ROW_TILE == 0 and t % IN_PROJ_ROW_TILE == 0
    assert seqlen % (S5_BLOCKS_PER_STEP * S5_BLOCK) == 0
    x2 = x.reshape(t, D_MODEL)

    zs, xbct, dt, dtt, u_slab, gates = _in_proj(x2, norm_mix_g, w_in)
    ya = _ssd(xbct, zs, dt, dtt, conv_w, conv_b, dt_bias, a_log, d_ssd, ssd_norm_g, bsz,
              seqlen)
    tables = _s5_tables(*_s5_prep(s5_a_re, s5_a_im, s5_log_dt, s5_b_re, s5_b_im,
                                  s5_c_re, s5_c_im), s5_d)
    yb_slab = _s5(u_slab, *tables, bsz)
    out = _merge_mlp(x2, ya, yb_slab, gates, s5_glu_w, s5_glu_b, w_branch, w_out, norm_mlp_g,
                     w_mlp_in, w_mlp_out, norm_final_g)
    return out.reshape(bsz, seqlen, D_MODEL)
```

```python
import jax
import jax.numpy as jnp
from jax import lax
from jax.experimental import pallas as pl
from jax.experimental.pallas import tpu as pltpu

F32 = jnp.float32
BF16 = jnp.bfloat16

D_MODEL = 1024
SSD_D_INNER = D_MODEL
SSD_HEADDIM = 64
SSD_N_HEADS = SSD_D_INNER // SSD_HEADDIM
SSD_N_GROUPS = 4
SSD_HPG = SSD_N_HEADS // SSD_N_GROUPS
SSD_D_STATE = 128
SSD_CONV = 4
SSD_CHUNK = 128
SSD_GROUP_W = SSD_D_INNER // SSD_N_GROUPS
SSD_BC_W = SSD_N_GROUPS * SSD_D_STATE
SSD_CONV_DIM = SSD_D_INNER + 2 * SSD_BC_W
S5_WIDTH = D_MODEL // 2
S5_GROUP = 16
S5_N_GROUPS = S5_WIDTH // S5_GROUP
S5_STATE = 64
S5_BLOCK = 16
S5_BLOCK_W = S5_BLOCK * S5_GROUP
D_FF = 4 * D_MODEL
N_BRANCHES = 2
OFF_Z = 0
OFF_XBC = OFF_Z + SSD_D_INNER
OFF_DT = OFF_XBC + SSD_CONV_DIM
OFF_U = OFF_DT + SSD_N_HEADS
OFF_G = OFF_U + S5_WIDTH
EPS = 1e-6

V7X_LANES = 128
V7X_SUBLANES = 8
V7X_VMEM_LIMIT_BYTES = 56 * 1024 * 1024

ROW_TILE = 512
MLP_CHUNK = 512
IN_PROJ_ROW_TILE = 1024
SSD_ROW_TILE = 1024
S5_BLOCKS_PER_STEP = 128
CONV_ROWS = 512
S5_STATE_PITCH = 24
S5_IL_STRIDE = 4


def _const_spec(shape):
    zeros = (0,) * len(shape)
    return pl.BlockSpec(shape, lambda *_: zeros, pipeline_mode=pl.Buffered(1))


def _rms(x):
    return x * lax.rsqrt(jnp.mean(x * x, axis=-1, keepdims=True) + EPS)


def _split3(x):
    hi = x.astype(BF16)
    r1 = x - hi.astype(F32)
    mid = r1.astype(BF16)
    lo = (r1 - mid.astype(F32)).astype(BF16)
    return hi, mid, lo


def _softplus(x):
    return jnp.maximum(x, 0.0) + jnp.log(1.0 + jnp.exp(-jnp.abs(x)))


GELU_C0 = 0.7978845608028654
GELU_C1 = GELU_C0 * 0.044715


def _gelu_tanh(x):
    h = 0.5 * x
    return h + h * jnp.tanh(x * (GELU_C0 + GELU_C1 * (x * x)))


def _silu(x):
    h = 0.5 * x
    return h + h * jnp.tanh(h)


def _s5_prep_kernel(are_ref, aim_ref, ldt_ref, bre_ref, bim_ref, cre_ref, cim_ref,
                    mt_ref, ere_ref, eim_ref, fre_ref, fim_ref, pwre_ref, pwim_ref):
    a_re = are_ref[...]
    a_im = aim_ref[...]
    dt = jnp.exp(ldt_ref[...])
    mag = jnp.exp(a_re * dt)
    ab_re = mag * jnp.cos(a_im * dt)
    ab_im = mag * jnp.sin(a_im * dt)
    den = a_re * a_re + a_im * a_im
    nr = ab_re - 1.0
    ni = ab_im
    coef_re = (nr * a_re + ni * a_im) / den
    coef_im = (ni * a_re - nr * a_im) / den
    b_re = bre_ref[...]
    b_im = bim_ref[...]
    bb_re = coef_re * b_re - coef_im * b_im
    bb_im = coef_re * b_im + coef_im * b_re
    c_re = cre_ref[...]
    c_im = cim_ref[...]

    p_re = jnp.ones_like(ab_re)
    p_im = jnp.zeros_like(ab_re)
    pb_re = [None] * S5_BLOCK
    pb_im = [None] * S5_BLOCK
    for tau in range(S5_BLOCK + 1):
        if tau < S5_BLOCK:
            s = S5_BLOCK - 1 - tau
            pb_re[s] = p_re * bb_re - p_im * bb_im
            pb_im[s] = p_re * bb_im + p_im * bb_re
            ere_ref[s] = pb_re[s]
            eim_ref[s] = pb_im[s]
        if tau >= 1:
            fre_ref[tau - 1] = c_re * p_re - c_im * p_im
            fim_ref[tau - 1] = -(c_re * p_im + c_im * p_re)
        if tau == S5_BLOCK:
            pwre_ref[...] = p_re
            pwim_ref[...] = p_im
        p_re, p_im = (p_re * ab_re - p_im * ab_im, p_re * ab_im + p_im * ab_re)

    dn = (((2,), (2,)), ((0,), (0,)))
    krev = (lax.dot_general(c_re, jnp.concatenate(pb_re, axis=1), dn,
                            precision=lax.Precision.HIGHEST, preferred_element_type=F32)
            - lax.dot_general(c_im, jnp.concatenate(pb_im, axis=1), dn,
                              precision=lax.Precision.HIGHEST, preferred_element_type=F32))
    lane = lax.broadcasted_iota(jnp.int32, krev.shape, 2)
    for t in range(S5_BLOCK):
        keep = S5_GROUP * (t + 1)
        moved = pltpu.roll(krev, keep % S5_BLOCK_W, 2)
        mt_ref[:, t * S5_GROUP:(t + 1) * S5_GROUP, :] = jnp.where(
            lane < keep, moved, 0.0).astype(BF16)


def _s5_prep(a_re, a_im, log_dt, b_re, b_im, c_re, c_im):
    g = S5_N_GROUPS
    bt_re = jnp.transpose(b_re, (0, 2, 1))
    bt_im = jnp.transpose(b_im, (0, 2, 1))
    tab = jax.ShapeDtypeStruct((S5_BLOCK, g, S5_GROUP, S5_STATE), F32)
    return pl.pallas_call(
        _s5_prep_kernel,
        out_shape=(jax.ShapeDtypeStruct((g, S5_BLOCK_W, S5_BLOCK_W), BF16), tab, tab, tab, tab,
                   jax.ShapeDtypeStruct((g, 1, S5_STATE), F32),
                   jax.ShapeDtypeStruct((g, 1, S5_STATE), F32)),
        name="s5_prep",
    )(a_re.reshape(g, 1, S5_STATE), a_im.reshape(g, 1, S5_STATE), log_dt.reshape(g, 1, 1),
      bt_re, bt_im, c_re, c_im)


def _in_proj_kernel(x_ref, g_ref, wz_ref, wxbct_ref, wdt_ref, wu_ref, wg_ref,
                    zs_ref, xbct_ref, dt_ref, dtt_ref, u_ref, gate_ref):
    h = (_rms(x_ref[...]) * g_ref[...]).astype(BF16)
    xbct_ref[...] = lax.dot_general(wxbct_ref[...], h, (((1,), (1,)), ((), ())),
                                    preferred_element_type=F32).astype(BF16)
    zs_ref[...] = _silu(jnp.dot(h, wz_ref[...], preferred_element_type=F32)).astype(BF16)
    u = jnp.dot(h, wu_ref[...], preferred_element_type=F32)
    for j in range(u_ref.shape[0]):
        u_ref[j] = u[:, j * V7X_LANES:(j + 1) * V7X_LANES]
    gate_ref[...] = jax.nn.sigmoid(
        jnp.dot(h, wg_ref[...], preferred_element_type=F32)).astype(BF16)
    dt = jnp.dot(h, wdt_ref[...], preferred_element_type=F32)
    dt_ref[...] = dt[:, :SSD_N_HEADS]
    for c in range(dtt_ref.shape[0]):
        dtt_ref[c] = dt[c * SSD_CHUNK:(c + 1) * SSD_CHUNK, :].T[:SSD_N_HEADS, :]


def _in_proj(x2, g_mix, w_in):
    t = x2.shape[0]
    tm = IN_PROJ_ROW_TILE
    wz = w_in[:, OFF_Z:OFF_XBC].astype(BF16)
    wxbct = w_in[:, OFF_XBC:OFF_DT].T.astype(BF16)
    wdt_pad = jnp.pad(w_in[:, OFF_DT:OFF_U],
                      ((0, 0), (0, V7X_LANES - SSD_N_HEADS))).astype(BF16)
    wu = w_in[:, OFF_U:OFF_G].astype(BF16)
    wg = w_in[:, OFF_G:].astype(BF16)
    n_slab = S5_WIDTH // V7X_LANES
    row = lambda w: pl.BlockSpec((tm, w), lambda i: (i, 0))
    return pl.pallas_call(
        _in_proj_kernel,
        grid=(t // tm,),
        in_specs=[row(D_MODEL), _const_spec((1, D_MODEL)), _const_spec(wz.shape),
                  _const_spec(wxbct.shape), _const_spec(wdt_pad.shape),
                  _const_spec(wu.shape), _const_spec(wg.shape)],
        out_specs=[row(SSD_D_INNER),
                   pl.BlockSpec((SSD_CONV_DIM, tm), lambda i: (0, i)),
                   row(SSD_N_HEADS),
                   pl.BlockSpec((tm // SSD_CHUNK, SSD_N_HEADS, SSD_CHUNK), lambda i: (i, 0, 0)),
                   pl.BlockSpec((n_slab, tm, V7X_LANES), lambda i: (0, i, 0)),
                   row(N_BRANCHES * D_MODEL)],
        out_shape=(jax.ShapeDtypeStruct((t, SSD_D_INNER), BF16),
                   jax.ShapeDtypeStruct((SSD_CONV_DIM, t), BF16),
                   jax.ShapeDtypeStruct((t, SSD_N_HEADS), F32),
                   jax.ShapeDtypeStruct((t // SSD_CHUNK, SSD_N_HEADS, SSD_CHUNK), F32),
                   jax.ShapeDtypeStruct((n_slab, t, V7X_LANES), F32),
                   jax.ShapeDtypeStruct((t, N_BRANCHES * D_MODEL), BF16)),
        compiler_params=pltpu.CompilerParams(dimension_semantics=("arbitrary",),
                                             vmem_limit_bytes=V7X_VMEM_LIMIT_BYTES),
        name="in_proj",
    )(x2, g_mix.reshape(1, D_MODEL), wz, wxbct, wdt_pad, wu, wg)


def _ssd_kernel(xbct_ref, zs_ref, dt_ref, dtt_ref, cwl_ref, cbl_ref, dtbc_ref, dtbr_ref,
                ac_ref, ar_ref, dcol_ref, ng_ref, y_ref, xprev, ht_s, st_s):
    tm = xbct_ref.shape[1]
    nch = tm // SSD_CHUNK
    q = SSD_CHUNK

    @pl.when(pl.program_id(1) == 0)
    def _():
        st_s[...] = jnp.zeros_like(st_s)
        xprev[...] = jnp.zeros_like(xprev)

    sh_r = lax.broadcasted_iota(jnp.int32, (2 * q, q), 0)
    sh_c = lax.broadcasted_iota(jnp.int32, (2 * q, q), 1)
    shift_all = jnp.concatenate(
        [(sh_r == sh_c + (q - (SSD_CONV - 1) + k)).astype(BF16) for k in range(SSD_CONV - 1)],
        axis=1)

    def conv_t(c, c0, n):
        if c == 0:
            win = jnp.concatenate([xprev[c0:c0 + n, :], xbct_ref[c0:c0 + n, 0:q]], axis=1)
        else:
            win = xbct_ref[c0:c0 + n, (c - 1) * q:(c + 1) * q]
        sh = jnp.dot(win, shift_all, preferred_element_type=F32)
        acc = cbl_ref[c0:c0 + n, :] + cwl_ref[SSD_CONV - 1, c0:c0 + n, :] * win[:, q:].astype(F32)
        for k in range(SSD_CONV - 1):
            acc = acc + cwl_ref[k, c0:c0 + n, :] * sh[:, k * q:(k + 1) * q]
        return acc + acc * jnp.tanh(acc)

    row_i = lax.broadcasted_iota(jnp.int32, (q, q), 0)
    col_i = lax.broadcasted_iota(jnp.int32, (q, q), 1)
    causal = row_i >= col_i
    tri = causal.astype(BF16)
    tri_t = (row_i <= col_i).astype(BF16)
    head_of_lane = lax.broadcasted_iota(jnp.int32, (q, SSD_GROUP_W), 1) // SSD_HEADDIM
    widen = (lax.broadcasted_iota(jnp.int32, (SSD_N_HEADS, SSD_D_INNER), 1) // SSD_HEADDIM
             == lax.broadcasted_iota(jnp.int32, (SSD_N_HEADS, SSD_D_INNER), 0)).astype(BF16)
    a_c = -jnp.exp(ac_ref[...])
    a_r = -jnp.exp(ar_ref[...])

    def chunk(c):
        r0 = c * q
        for c0 in range(0, SSD_CONV_DIM, CONV_ROWS):
            ht_s[c0:c0 + CONV_ROWS, :] = conv_t(c, c0, CONV_ROWS)
        dt_c = _softplus(dt_ref[pl.ds(r0, q), :] + dtbc_ref[...])
        dt_r = _softplus(dtt_ref[c] + dtbr_ref[...])
        la_c = dt_c * a_c
        la_r = dt_r * a_r
        cum_c = sum(jnp.dot(tri, p, preferred_element_type=F32) for p in _split3(la_c))
        cum_r = sum(jnp.dot(p, tri_t, preferred_element_type=F32) for p in _split3(la_r))
        cum_rl = cum_r - jnp.log(dt_r)
        to_end = dt_c * jnp.exp(cum_c[q - 1:q, :] - cum_c)
        te_x = jnp.dot(to_end.astype(BF16), widen, preferred_element_type=F32)
        ec_x = jnp.dot(jnp.exp(cum_c).astype(BF16), widen, preferred_element_type=F32)

        for g in range(SSD_N_GROUPS):
            b0 = SSD_D_INNER + g * SSD_D_STATE
            c0 = SSD_D_INNER + SSD_BC_W + g * SSD_D_STATE
            bgt = ht_s[b0:b0 + SSD_D_STATE, :].astype(BF16)
            cg = ht_s[c0:c0 + SSD_D_STATE, :].T.astype(BF16)
            cb = jnp.dot(cg, bgt, preferred_element_type=F32)
            ws = []
            for r in range(SSD_HPG):
                h = g * SSD_HPG + r
                cc = jnp.broadcast_to(cum_c[:, h:h + 1], (q, q))
                cr = jnp.broadcast_to(cum_rl[h:h + 1, :], (q, q))
                lm = jnp.exp(jnp.where(causal, cc - cr, -jnp.inf))
                ws.append((cb * lm).astype(BF16))
            wcat = jnp.concatenate(ws, axis=1)
            cols = slice(g * SSD_GROUP_W, (g + 1) * SSD_GROUP_W)
            xs_g = ht_s[g * SSD_GROUP_W:(g + 1) * SSD_GROUP_W, :].T
            xs_b = xs_g.astype(BF16)
            xdtd = (xs_g * te_x[:, cols]).astype(BF16)
            xbd = jnp.concatenate(
                [jnp.where(head_of_lane == r, xs_b, jnp.zeros_like(xs_b))
                 for r in range(SSD_HPG)], axis=0)
            y = jnp.dot(wcat, xbd, preferred_element_type=F32)
            prev = st_s[:, cols]
            y = y + jnp.dot(cg, prev.astype(BF16), preferred_element_type=F32) * ec_x[:, cols]
            st_s[:, cols] = prev * ec_x[q - 1:q, cols] + jnp.dot(
                bgt, xdtd, preferred_element_type=F32)
            y = y + dcol_ref[:, cols] * xs_g
            y = y * zs_ref[pl.ds(r0, q), cols].astype(F32)
            y_ref[pl.ds(r0, q), cols] = (_rms(y) * ng_ref[:, cols]).astype(BF16)

    for c in range(nch):
        chunk(c)
    xprev[...] = xbct_ref[:, tm - q:tm]


def _ssd(xbct, zs, dt, dtt, conv_w, conv_b, dt_bias, a_log, d_ssd, ssd_norm_g, bsz, seqlen):
    tm = SSD_ROW_TILE
    nt = seqlen // tm
    nch = tm // SSD_CHUNK
    h = SSD_N_HEADS
    row = lambda w: pl.BlockSpec((tm, w), lambda b, i: (b * nt + i, 0))
    cwl = jnp.broadcast_to((0.5 * conv_w)[:, :, None], (SSD_CONV, SSD_CONV_DIM, V7X_LANES))
    cbl = jnp.broadcast_to((0.5 * conv_b)[:, None], (SSD_CONV_DIM, V7X_LANES))
    return pl.pallas_call(
        _ssd_kernel,
        grid=(bsz, nt),
        in_specs=[pl.BlockSpec((SSD_CONV_DIM, tm), lambda b, i: (0, b * nt + i)),
                  row(SSD_D_INNER), row(h),
                  pl.BlockSpec((nch, h, SSD_CHUNK), lambda b, i: (b * nt + i, 0, 0)),
                  _const_spec(cwl.shape), _const_spec(cbl.shape),
                  _const_spec((1, h)), _const_spec((h, 1)), _const_spec((1, h)),
                  _const_spec((h, 1)), _const_spec((1, SSD_D_INNER)),
                  _const_spec((1, SSD_D_INNER))],
        out_specs=row(SSD_D_INNER),
        out_shape=jax.ShapeDtypeStruct((bsz * seqlen, SSD_D_INNER), BF16),
        scratch_shapes=[pltpu.VMEM((SSD_CONV_DIM, SSD_CHUNK), BF16),
                        pltpu.VMEM((SSD_CONV_DIM, SSD_CHUNK), F32),
                        pltpu.VMEM((SSD_D_STATE, SSD_D_INNER), F32)],
        compiler_params=pltpu.CompilerParams(dimension_semantics=("arbitrary", "arbitrary"),
                                             vmem_limit_bytes=V7X_VMEM_LIMIT_BYTES),
        name="ssd",
    )(xbct, zs, dt, dtt, cwl, cbl, dt_bias.reshape(1, h), dt_bias.reshape(h, 1),
      a_log.reshape(1, h), a_log.reshape(h, 1),
      jnp.repeat(d_ssd, SSD_HEADDIM).reshape(1, SSD_D_INNER), ssd_norm_g.reshape(1, -1))


def _s5_kernel(u_ref, mt_ref, et_ref, ft_ref, pre_ref, pim_ref, d_ref, y_ref,
               ut_s, yt_s, il_s, sre, sim, cre, cim):
    n_slab = u_ref.shape[0]
    nblk = u_ref.shape[1] // S5_BLOCK
    npair = S5_N_GROUPS // 2
    gps = V7X_LANES // S5_GROUP
    hw = V7X_LANES

    @pl.when(pl.program_id(1) == 0)
    def _():
        cre[...] = jnp.zeros_like(cre)
        cim[...] = jnp.zeros_like(cim)

    sub = S5_IL_STRIDE
    for j in range(n_slab):
        for c in range(sub):
            il_s[j, c] = u_ref[j, pl.ds(c, nblk * sub, stride=sub), :]
    for s in range(S5_BLOCK):
        for j in range(n_slab):
            xt = il_s[j, s % sub, pl.ds(s // sub, nblk, stride=sub), :].astype(BF16).T
            for gl in range(gps):
                ut_s[j * gps + gl, s * S5_GROUP:(s + 1) * S5_GROUP, :] = (
                    xt[gl * S5_GROUP:(gl + 1) * S5_GROUP, :])

    for pr in range(npair):
        ucat = jnp.concatenate([ut_s[2 * pr], ut_s[2 * pr + 1]], axis=0)
        sl = jnp.dot(et_ref[pr], ucat, preferred_element_type=F32).T
        sre[pl.ds(pr, nblk, stride=S5_STATE_PITCH), :] = sl[:, :hw]
        sim[pl.ds(pr, nblk, stride=S5_STATE_PITCH), :] = sl[:, hw:]

    a_re = pre_ref[...]
    a_im = pim_ref[...]

    def step(r, carry):
        c_re, c_im = carry
        rows = pl.ds(pl.multiple_of(r * S5_STATE_PITCH, V7X_SUBLANES), npair)
        l_re = sre[rows, :]
        l_im = sim[rows, :]
        sre[rows, :] = c_re
        sim[rows, :] = c_im
        return (a_re * c_re - a_im * c_im + l_re, a_re * c_im + a_im * c_re + l_im)

    c_re, c_im = lax.fori_loop(0, nblk, step, (cre[...], cim[...]))
    cre[...] = c_re
    cim[...] = c_im

    for pr in range(npair):
        sp = jnp.concatenate([sre[pl.ds(pr, nblk, stride=S5_STATE_PITCH), :],
                              sim[pl.ds(pr, nblk, stride=S5_STATE_PITCH), :]], axis=1).astype(BF16)
        cross = lax.dot_general(ft_ref[pr], sp, (((1,), (1,)), ((), ())),
                                preferred_element_type=F32)
        for j in range(2):
            gi = 2 * pr + j
            ug = ut_s[gi]
            y = (jnp.dot(mt_ref[gi], ug, preferred_element_type=F32)
                 + cross[j * S5_BLOCK_W:(j + 1) * S5_BLOCK_W, :]
                 + d_ref[gi] * ug.astype(F32))
            yt_s[gi] = _gelu_tanh(y).astype(BF16)

    for t in range(S5_BLOCK):
        for j in range(n_slab):
            zt = jnp.concatenate(
                [yt_s[j * gps + gl, t * S5_GROUP:(t + 1) * S5_GROUP, :] for gl in range(gps)],
                axis=0)
            il_s[j, t % sub, pl.ds(t // sub, nblk, stride=sub), :] = zt.T.astype(F32)
    for j in range(n_slab):
        for c in range(sub):
            y_ref[j, pl.ds(c, nblk * sub, stride=sub), :] = il_s[j, c]


def _s5(u_slab, mt, et, ft, pw_re, pw_im, d_col, bsz):
    n_slab, t, _ = u_slab.shape
    nblk = S5_BLOCKS_PER_STEP
    rows = nblk * S5_BLOCK
    parts = t // bsz // rows
    g, w = S5_N_GROUPS, S5_BLOCK_W
    io = pl.BlockSpec((n_slab, rows, V7X_LANES), lambda b, p: (0, b * parts + p, 0))
    state = pltpu.VMEM((nblk * S5_STATE_PITCH, V7X_LANES), F32)
    carry = pltpu.VMEM((g // 2, V7X_LANES), F32)
    return pl.pallas_call(
        _s5_kernel,
        grid=(bsz, parts),
        in_specs=[io, _const_spec(mt.shape), _const_spec(et.shape), _const_spec(ft.shape),
                  _const_spec(pw_re.shape), _const_spec(pw_im.shape), _const_spec(d_col.shape)],
        out_specs=io,
        out_shape=jax.ShapeDtypeStruct(u_slab.shape, F32),
        scratch_shapes=[pltpu.VMEM((g, w, nblk), BF16), pltpu.VMEM((g, w, nblk), BF16),
                        pltpu.VMEM((n_slab, S5_IL_STRIDE, rows // S5_IL_STRIDE, V7X_LANES), F32),
                        state, state, carry, carry],
        compiler_params=pltpu.CompilerParams(dimension_semantics=("arbitrary", "arbitrary"),
                                             vmem_limit_bytes=V7X_VMEM_LIMIT_BYTES),
        name="s5",
    )(u_slab, mt, et, ft, pw_re, pw_im, d_col)


def _s5_tables(mt, e_re, e_im, f_re, f_im, pw_re, pw_im, s5_d):
    g, q, w = S5_N_GROUPS, S5_BLOCK, S5_BLOCK_W
    eg_re = jnp.transpose(e_re, (1, 0, 2, 3)).reshape(g // 2, 2, w, S5_STATE)
    eg_im = jnp.transpose(e_im, (1, 0, 2, 3)).reshape(g // 2, 2, w, S5_STATE)
    zero = jnp.zeros_like(eg_re[:, 0])
    e_top = jnp.concatenate([eg_re[:, 0], zero, eg_im[:, 0], zero], axis=-1)
    e_bot = jnp.concatenate([zero, eg_re[:, 1], zero, eg_im[:, 1]], axis=-1)
    e = jnp.concatenate([e_top, e_bot], axis=1)
    fg_re = jnp.transpose(f_re, (1, 3, 0, 2)).reshape(g // 2, 2, S5_STATE, w)
    fg_im = jnp.transpose(f_im, (1, 3, 0, 2)).reshape(g // 2, 2, S5_STATE, w)
    zf = jnp.zeros_like(fg_re[:, 0])
    f = jnp.concatenate([
        jnp.concatenate([fg_re[:, 0], zf], axis=-1),
        jnp.concatenate([zf, fg_re[:, 1]], axis=-1),
        jnp.concatenate([fg_im[:, 0], zf], axis=-1),
        jnp.concatenate([zf, fg_im[:, 1]], axis=-1)], axis=1)
    tr = lambda a: jnp.swapaxes(a, 1, 2).astype(BF16)
    pw_re = pw_re.reshape(g // 2, 2 * S5_STATE)
    pw_im = pw_im.reshape(g // 2, 2 * S5_STATE)
    d_col = jnp.tile(s5_d.reshape(g, 1, S5_GROUP), (1, q, 1)).reshape(g, w, 1)
    return mt, tr(e), tr(f), pw_re, pw_im, d_col


def _merge_mlp_kernel(x_ref, ya_ref, yb_ref, gate_ref, gw_ref, gb_ref, wa_ref, wb_ref, wo_ref,
                      g2_ref, wi_ref, wo2_ref, g3_ref, o_ref):
    ybf = jnp.concatenate([yb_ref[j] for j in range(yb_ref.shape[0])], axis=1)
    yb = ybf.astype(BF16)
    glu = jnp.dot(yb, gw_ref[...], preferred_element_type=F32) + gb_ref[...]
    ybg = (ybf * jax.nn.sigmoid(glu)).astype(BF16)
    pa = jnp.dot(ya_ref[...], wa_ref[...], preferred_element_type=F32)
    pb = jnp.dot(ybg, wb_ref[...], preferred_element_type=F32)
    merged = (gate_ref[:, :D_MODEL].astype(F32) * pa
              + gate_ref[:, D_MODEL:].astype(F32) * pb).astype(BF16)
    x1 = x_ref[...] + jnp.dot(merged, wo_ref[...], preferred_element_type=F32)
    h2 = (_rms(x1) * g2_ref[...]).astype(BF16)
    acc = x1
    for k in range(D_FF // MLP_CHUNK):
        cols = slice(k * MLP_CHUNK, (k + 1) * MLP_CHUNK)
        hid = jnp.dot(h2, wi_ref[:, cols], preferred_element_type=F32)
        hid = jnp.square(jnp.maximum(hid, 0.0)).astype(BF16)
        acc = acc + jnp.dot(hid, wo2_ref[cols, :], preferred_element_type=F32)
    o_ref[...] = _rms(acc) * g3_ref[...]


def _merge_mlp(x2, ya, yb_slab, gates, glu_w, glu_b, w_branch, w_out, g_mlp, w_mlp_in,
               w_mlp_out, g_final):
    t = x2.shape[0]
    tm = ROW_TILE
    wa = w_branch[:SSD_D_INNER].astype(BF16)
    wb = w_branch[SSD_D_INNER:].astype(BF16)
    row = lambda w: pl.BlockSpec((tm, w), lambda i: (i, 0))
    vec = lambda v: v.reshape(1, -1)
    return pl.pallas_call(
        _merge_mlp_kernel,
        grid=(t // tm,),
        in_specs=[row(D_MODEL), row(SSD_D_INNER),
                  pl.BlockSpec((yb_slab.shape[0], tm, V7X_LANES), lambda i: (0, i, 0)),
                  row(N_BRANCHES * D_MODEL),
                  _const_spec((S5_WIDTH, S5_WIDTH)), _const_spec((1, S5_WIDTH)),
                  _const_spec(wa.shape), _const_spec(wb.shape), _const_spec((D_MODEL, D_MODEL)),
                  _const_spec((1, D_MODEL)), _const_spec((D_MODEL, D_FF)),
                  _const_spec((D_FF, D_MODEL)), _const_spec((1, D_MODEL))],
        out_specs=row(D_MODEL),
        out_shape=jax.ShapeDtypeStruct((t, D_MODEL), F32),
        compiler_params=pltpu.CompilerParams(dimension_semantics=("arbitrary",),
                                             vmem_limit_bytes=V7X_VMEM_LIMIT_BYTES),
        name="merge_mlp",
    )(x2, ya, yb_slab, gates, glu_w.astype(BF16), vec(glu_b), wa, wb, w_out.astype(BF16),
      vec(g_mlp), w_mlp_in.astype(BF16), w_mlp_out.astype(BF16), vec(g_final))


def kernel(x, norm_mix_g, w_in, conv_w, conv_b, dt_bias, a_log, d_ssd, ssd_norm_g, s5_a_re, s5_a_im, s5_log_dt, s5_b_re, s5_b_im, s5_c_re, s5_c_im, s5_d, s5_glu_w, s5_glu_b, w_branch, w_out, norm_mlp_g, w_mlp_in, w_mlp_out, norm_final_g):
    bsz, seqlen, _ = x.shape
    t = bsz * seqlen
    assert seqlen % SSD_ROW_TILE == 0 and t % ROW_TILE == 0 and t % IN_PROJ_ROW_TILE == 0
    assert seqlen % (S5_BLOCKS_PER_STEP * S5_BLOCK) == 0
    x2 = x.reshape(t, D_MODEL)

    zs, xbct, dt, dtt, u_slab, gates = _in_proj(x2, norm_mix_g, w_in)
    ya = _ssd(xbct, zs, dt, dtt, conv_w, conv_b, dt_bias, a_log, d_ssd, ssd_norm_g, bsz,
              seqlen)
    tables = _s5_tables(*_s5_prep(s5_a_re, s5_a_im, s5_log_dt, s5_b_re, s5_b_im,
                                  s5_c_re, s5_c_im), s5_d)
    yb_slab = _s5(u_slab, *tables, bsz)
    out = _merge_mlp(x2, ya, yb_slab, gates, s5_glu_w, s5_glu_b, w_branch, w_out, norm_mlp_g,
                     w_mlp_in, w_mlp_out, norm_final_g)
    return out.reshape(bsz, seqlen, D_MODEL)
```
